```python
import math
import jax, jax.numpy as jnp
from jax import lax
import numpy as np

D_MODEL = 2048
BATCH = 1
SEQ = 8192
DEPTH = 1

CHUNK = 64
EPS = 1e-6
RW_HEADS = 16
RW_HEAD_DIM = 64
RW_WIDTH = RW_HEADS * RW_HEAD_DIM
RW_DECAY_RANK = 64
RW_ICL_RANK = 64
RW_GATE_RANK = 160
RW_GN_EPS = 64e-5
RW_SPLITS = (RW_WIDTH, RW_WIDTH, RW_WIDTH, RW_DECAY_RANK, RW_ICL_RANK, RW_GATE_RANK)
RW_IN = 3 * RW_WIDTH + RW_DECAY_RANK + RW_ICL_RANK + RW_GATE_RANK
SA_HEADS = 16
SA_HEAD_DIM = 64
SA_WIDTH = SA_HEADS * SA_HEAD_DIM
SA_SCALE = SA_HEAD_DIM ** -0.5
KV_RANK = 256
IDX_HEADS = 8
IDX_DIM = 64
IDX_SCALE = (IDX_HEADS * IDX_DIM) ** -0.5
TOPK_MAX = 256
Q_BLOCK = 128
REL_BUCKETS = 32
REL_MAX_DIST = 1024
N_BRANCHES = 2
IN_SPLITS = (RW_IN, SA_WIDTH, KV_RANK, IDX_HEADS * IDX_DIM, IDX_DIM, IDX_HEADS, N_BRANCHES * D_MODEL)
IN_TOTAL = RW_IN + SA_WIDTH + KV_RANK + IDX_HEADS * IDX_DIM + IDX_DIM + IDX_HEADS + N_BRANCHES * D_MODEL
PEER_HEADS = 8
PEER_KEYS = 128
PEER_EXPERTS = PEER_KEYS * PEER_KEYS
PEER_QDIM = 256
PEER_HALF = PEER_QDIM // 2
PEER_TOPK = 16
PEER_BLOCK = 128

kernel_name = 'hybrid_rwkv7_dsa_peer'


def _split(x, sizes):
    idx = []
    acc = 0
    for s in sizes[:-1]:
        acc += s
        idx.append(acc)
    return jnp.split(x, idx, axis=-1)


def _rmsnorm(x, g):
    x32 = x.astype(jnp.float32)
    y = x32 * lax.rsqrt(jnp.mean(x32 * x32, axis=-1, keepdims=True) + EPS)
    return (y * g.astype(jnp.float32)).astype(x.dtype)


def _rel_bucket(rel):
    nb = REL_BUCKETS // 2
    max_exact = nb // 2
    ret = (rel > 0).astype(jnp.int32) * nb
    n = jnp.abs(rel)
    large = max_exact + (jnp.log(jnp.maximum(n, 1).astype(jnp.float32) / max_exact)
                         / math.log(REL_MAX_DIST / max_exact) * (nb - max_exact)).astype(jnp.int32)
    large = jnp.minimum(large, nb - 1)
    return ret + jnp.where(n < max_exact, n, large)


def _rwkv7_time_mix(p, shift_mix, w0, w_decay_up, a0, w_icl_up, w_gate_up, k_k, k_a, r_k, ln_x_w, ln_x_b):
    B, T, _ = p.shape
    H, N = RW_HEADS, RW_HEAD_DIM
    f32 = jnp.float32
    prev = jnp.pad(p, ((0, 0), (1, 0), (0, 0)))[:, :T]
    z = p + (prev - p) * shift_mix
    r, k, v, zw, za, zg = _split(z, RW_SPLITS)
    w_log = -jax.nn.softplus(-(w0 + jnp.tanh(zw) @ w_decay_up).astype(f32)) - 0.5
    decay = jnp.exp(-jnp.exp(w_log))
    a = jax.nn.sigmoid((a0 + za @ w_icl_up).astype(f32))
    g = jax.nn.sigmoid(zg) @ w_gate_up
    kf = k.astype(f32)
    kk = (kf * k_k.astype(f32)).reshape(B, T, H, N)
    kk = kk / jnp.maximum(jnp.linalg.norm(kk, axis=-1, keepdims=True), 1e-12)
    k4 = (kf * (1.0 + (a - 1.0) * k_a.astype(f32))).reshape(B, T, H, N)
    r4 = r.astype(f32).reshape(B, T, H, N)
    v4 = v.astype(f32).reshape(B, T, H, N)
    a4 = a.reshape(B, T, H, N)
    w4 = decay.reshape(B, T, H, N)

    def step(S, inp):
        r_t, k_t, v_t, kk_t, a_t, w_t = inp
        sa = jnp.einsum('bhvk,bhk->bhv', S, -kk_t)
        S = (S * w_t[:, :, None, :] + sa[..., None] * (kk_t * a_t)[:, :, None, :]
             + v_t[..., None] * k_t[:, :, None, :])
        return S, jnp.einsum('bhvk,bhk->bhv', S, r_t)

    xs = tuple(jnp.moveaxis(t, 1, 0) for t in (r4, k4, v4, kk, a4, w4))
    _, ys = lax.scan(step, jnp.zeros((B, H, N, N), f32), xs)
    y = jnp.moveaxis(ys, 0, 1)
    mean = jnp.mean(y, axis=-1, keepdims=True)
    var = jnp.mean(jnp.square(y - mean), axis=-1, keepdims=True)
    y = ((y - mean) * lax.rsqrt(var + RW_GN_EPS)).reshape(B, T, RW_WIDTH)
    y = y * ln_x_w.astype(f32) + ln_x_b.astype(f32)
    bonus = jnp.sum(r4 * k4 * r_k.astype(f32), axis=-1, keepdims=True) * v4
    y = (y + bonus.reshape(B, T, RW_WIDTH)) * g.astype(f32)
    return y.astype(p.dtype)


def _dsa_attention(q, c_kv, iq, ik, iw, kv_norm_g, w_uk, w_uv, rel_bias):
    B, T, _ = q.shape
    topk = min(TOPK_MAX, T // 4)
    f32 = jnp.float32
    c = _rmsnorm(c_kv, kv_norm_g)
    q_lat = jnp.einsum('bthd,hrd->bthr', q.reshape(B, T, SA_HEADS, SA_HEAD_DIM), w_uk)
    iq = iq.reshape(B, T, IDX_HEADS, IDX_DIM)
    iw = iw * IDX_SCALE
    key_chunk = jnp.arange(T) // CHUNK

    def block(i):
        s0 = i * Q_BLOCK
        qi = lax.dynamic_slice_in_dim(iq, s0, Q_BLOCK, axis=1)
        wi = lax.dynamic_slice_in_dim(iw, s0, Q_BLOCK, axis=1)
        ql = lax.dynamic_slice_in_dim(q_lat, s0, Q_BLOCK, axis=1)
        tq = s0 + jnp.arange(Q_BLOCK)
        score = jnp.einsum('bqhs,bqh->bqs', jax.nn.relu(jnp.einsum('bqhd,bsd->bqhs', qi, ik)), wi).astype(f32)
        allowed = key_chunk[None, :] <= (tq // CHUNK)[:, None]
        score = jnp.where(allowed[None], score, -jnp.inf)
        top_s, idx = lax.top_k(score, topk)
        valid = jnp.isfinite(top_s)
        c_sel = jax.vmap(lambda cb, ib: cb[ib])(c, idx)
        logits = jnp.einsum('bqhr,bqkr->bqhk', ql, c_sel).astype(f32) * SA_SCALE
        bias = rel_bias[_rel_bucket(idx - tq[None, :, None])]
        logits = logits + jnp.transpose(bias, (0, 1, 3, 2)).astype(f32)
        logits = jnp.where(valid[:, :, None, :], logits, -jnp.inf)
        pr = jax.nn.softmax(logits, axis=-1).astype(c.dtype)
        o_lat = jnp.einsum('bqhk,bqkr->bqhr', pr, c_sel)
        o = jnp.einsum('bqhr,hrd->bqhd', o_lat, w_uv)
        return o.reshape(B, Q_BLOCK, SA_WIDTH)

    out = lax.map(block, jnp.arange(T // Q_BLOCK))
    return jnp.moveaxis(out, 0, 1).reshape(B, T, SA_WIDTH)


def _peer_ffn(h, w_query, sub_keys, expert_u, expert_v):
    B, T, D = h.shape
    q = (h @ w_query).reshape(B, T, PEER_HEADS, 2, PEER_HALF)
    s = jnp.einsum('bthcq,hcnq->bthcn', q, sub_keys).astype(jnp.float32)
    s1, i1 = lax.top_k(s[..., 0, :], PEER_TOPK)
    s2, i2 = lax.top_k(s[..., 1, :], PEER_TOPK)
    cand = (s1[..., :, None] + s2[..., None, :]).reshape(B, T, PEER_HEADS, PEER_TOPK * PEER_TOPK)
    cand_id = (i1[..., :, None] * PEER_KEYS + i2[..., None, :]).reshape(B, T, PEER_HEADS, PEER_TOPK * PEER_TOPK)
    top_s, pos = lax.top_k(cand, PEER_TOPK)
    eid = jnp.take_along_axis(cand_id, pos, axis=-1)
    gate = jax.nn.softmax(top_s, axis=-1).astype(h.dtype)
    nb = (B * T) // PEER_BLOCK
    hb = h.reshape(nb, PEER_BLOCK, D)
    eb = eid.reshape(nb, PEER_BLOCK, PEER_HEADS, PEER_TOPK)
    gb = gate.reshape(nb, PEER_BLOCK, PEER_HEADS, PEER_TOPK)

    def blk(args):
        hx, e, gt = args
        act = jax.nn.gelu(jnp.einsum('nd,nhkd->nhk', hx, expert_u[e]), approximate=False)
        return jnp.einsum('nhk,nhkd->nd', gt * act, expert_v[e])

    return lax.map(blk, (hb, eb, gb)).reshape(B, T, D)


def setup_inputs(seed: int = 0) -> dict:
    key = jax.random.key(seed)
    ks = iter(jax.random.split(key, 40))
    L, D = DEPTH, D_MODEL
    nrm = lambda shape, scale: jax.random.normal(next(ks), shape, jnp.float32) * scale
    uni = lambda shape, lo, hi: jax.random.uniform(next(ks), shape, jnp.float32, lo, hi)
    gain = lambda shape: 1.0 + nrm(shape, 0.02)
    return {
        'x': nrm((BATCH, SEQ, D), 1.0),
        'norm_mix_g': gain((L, D)),
        'w_in': nrm((L, D, IN_TOTAL), D ** -0.5),
        'shift_mix': uni((L, RW_IN), 0.0, 1.0),
        'w0': uni((L, RW_WIDTH), -4.0, 0.0),
        'w_decay_up': nrm((L, RW_DECAY_RANK, RW_WIDTH), 0.3 * RW_DECAY_RANK ** -0.5),
        'a0': nrm((L, RW_WIDTH), 0.3),
        'w_icl_up': nrm((L, RW_ICL_RANK, RW_WIDTH), RW_ICL_RANK ** -0.5),
        'w_gate_up': nrm((L, RW_GATE_RANK, RW_WIDTH), RW_GATE_RANK ** -0.5),
        'k_k': 0.85 + nrm((L, RW_WIDTH), 0.05),
        'k_a': 1.0 + nrm((L, RW_WIDTH), 0.05),
        'r_k': nrm((L, RW_HEADS, RW_HEAD_DIM), 0.1),
        'ln_x_w': gain((L, RW_WIDTH)),
        'ln_x_b': nrm((L, RW_WIDTH), 0.02),
        'kv_norm_g': gain((L, KV_RANK)),
        'w_uk': nrm((L, SA_HEADS, KV_RANK, SA_HEAD_DIM), KV_RANK ** -0.5),
        'w_uv': nrm((L, SA_HEADS, KV_RANK, SA_HEAD_DIM), KV_RANK ** -0.5),
        'rel_bias': nrm((REL_BUCKETS, SA_HEADS), 0.5),
        'w_branch_rwkv': nrm((L, RW_WIDTH, D), RW_WIDTH ** -0.5),
        'w_branch_dsa': nrm((L, SA_WIDTH, D), SA_WIDTH ** -0.5),
        'w_out': nrm((L, D, D), D ** -0.5),
        'norm_ffn_g': gain((L, D)),
        'w_peer_query': nrm((L, D, PEER_HEADS * PEER_QDIM), D ** -0.5),
        'peer_sub_keys': nrm((L, PEER_HEADS, 2, PEER_KEYS, PEER_HALF), PEER_HALF ** -0.5),
        'peer_u': nrm((L, PEER_EXPERTS, D), D ** -0.5),
        'peer_v': nrm((L, PEER_EXPERTS, D), 0.25),
        'norm_final_g': gain((D,)),
    }


def reference(x, norm_mix_g, w_in, shift_mix, w0, w_decay_up, a0, w_icl_up, w_gate_up, k_k, k_a, r_k,
              ln_x_w, ln_x_b, kv_norm_g, w_uk, w_uv, rel_bias, w_branch_rwkv, w_branch_dsa, w_out,
              norm_ffn_g, w_peer_query, peer_sub_keys, peer_u, peer_v, norm_final_g):
    h = x
    for l in range(DEPTH):
        xn = _rmsnorm(h, norm_mix_g[l])
        p_rw, q, c_kv, iq, ik, iw, gates = _split(xn @ w_in[l], IN_SPLITS)
        y_a = _rwkv7_time_mix(p_rw, shift_mix[l], w0[l], w_decay_up[l], a0[l], w_icl_up[l], w_gate_up[l],
                              k_k[l], k_a[l], r_k[l], ln_x_w[l], ln_x_b[l]) @ w_branch_rwkv[l]
        y_b = _dsa_attention(q, c_kv, iq, ik, iw, kv_norm_g[l], w_uk[l], w_uv[l], rel_bias) @ w_branch_dsa[l]
        g_a, g_b = jnp.split(jax.nn.sigmoid(gates), N_BRANCHES, axis=-1)
        h = h + (g_a * y_a + g_b * y_b) @ w_out[l]
        h = h + _peer_ffn(_rmsnorm(h, norm_ffn_g[l]), w_peer_query[l], peer_sub_keys[l], peer_u[l], peer_v[l])
    return _rmsnorm(h, norm_final_g)
```

```python
import functools
import math

import numpy as np
import jax
import jax.numpy as jnp
from jax import lax
from jax.experimental import pallas as pl
from jax.experimental.pallas import tpu as pltpu

F32, BF16, I32 = jnp.float32, jnp.bfloat16, jnp.int32

D_MODEL = 2048
EPS = 1e-6
RW_WIDTH = 1024
RW_GN_EPS = 64e-5
RW_CHUNK = 128
SA_HEADS = 16
SA_SCALE = 64 ** -0.5
KV_RANK = 256
IDX_HEADS = 8
IDX_SCALE = (8 * 64) ** -0.5
TOPK_MAX = 256
REL_BUCKETS = 32
REL_MAX_DIST = 1024
PEER_HEADS = 8
PEER_KEYS = 128
PEER_TOPK = 16
LANES = 128
SUBLANES = 8
NEG_BIG = -1e30
INT_MIN = -2 ** 31

Q_BLK = 128
K_TILE = 256
N_BIAS_TILES = 8


def _cp(sem, vmem_mb=48):
    return pltpu.CompilerParams(dimension_semantics=sem, vmem_limit_bytes=vmem_mb * 1024 * 1024)


def _split2(x):
    hi = x.astype(BF16)
    lo = (x - hi.astype(F32)).astype(BF16)
    return hi, lo


def _mm(a, b):
    return jnp.dot(a, b, preferred_element_type=F32)


def _mm_nt(a, b):
    return lax.dot_general(a, b, (((1,), (1,)), ((), ())), preferred_element_type=F32)


def _mm_tn(a, b):
    return lax.dot_general(a, b, (((0,), (0,)), ((), ())), preferred_element_type=F32)


def _d3(a_s, b_s, mm=_mm):
    return mm(a_s[0], b_s[0]) + (mm(a_s[0], b_s[1]) + mm(a_s[1], b_s[0]))


def _dot3(a, b, mm=_mm):
    return _d3(_split2(a), _split2(b), mm)


def _proj_kernel(x_ref, g_ref, w_ref, o_ref, xh_ref, xl_ref, *, precise, lane_groups):
    @pl.when(pl.program_id(1) == 0)
    def _():
        x = x_ref[...]
        xn = x * lax.rsqrt(jnp.mean(x * x, axis=-1, keepdims=True) + EPS) * g_ref[...]
        hi = xn.astype(BF16)
        xh_ref[...] = hi
        if precise:
            xl_ref[...] = (xn - hi.astype(F32)).astype(BF16)

    if precise:
        wh, wl = _split2(w_ref[...])
        xh = xh_ref[...]
        res = _mm(xh, wh) + (_mm(xh, wl) + _mm(xl_ref[...], wh))
    else:
        res = _mm(xh_ref[...], w_ref[...])
    if lane_groups:
        for k in range(res.shape[1] // LANES):
            o_ref[k] = res[:, k * LANES:(k + 1) * LANES]
    else:
        o_ref[...] = res


def _proj(x, g, w, *, precise, tm, tn, name, lane_groups=False):
    T, D = x.shape
    N = w.shape[1]
    lo_rows = tm if precise else SUBLANES * 2
    if lane_groups:
        out_spec = pl.BlockSpec((tn // LANES, tm, LANES), lambda i, j: (j, i, 0))
        out_shape = jax.ShapeDtypeStruct((N // LANES, T, LANES), F32)
    else:
        out_spec = pl.BlockSpec((tm, tn), lambda i, j: (i, j))
        out_shape = jax.ShapeDtypeStruct((T, N), F32)
    return pl.pallas_call(
        functools.partial(_proj_kernel, precise=precise, lane_groups=lane_groups),
        grid=(T // tm, N // tn),
        in_specs=[pl.BlockSpec((tm, D), lambda i, j: (i, 0)),
                  pl.BlockSpec((1, D), lambda i, j: (0, 0)),
                  pl.BlockSpec((D, tn), lambda i, j: (0, j))],
        out_specs=out_spec,
        out_shape=out_shape,
        scratch_shapes=[pltpu.VMEM((tm, D), BF16), pltpu.VMEM((lo_rows, D), BF16)],
        compiler_params=_cp(("parallel", "arbitrary")),
        name=name,
    )(x, g.reshape(1, D), w)


def _rw_prep_kernel(r_ref, k_ref, v_ref, wa_ref, zg_ref, rp_ref, kp_ref, vp_ref, wap_ref, zgp_ref,
                    mr_ref, mk_ref, mv_ref, mwa_ref, mg_ref, w0_ref, wd_ref, a0_ref, wi_ref, wg_ref,
                    kk_ref, ka_ref, rk_ref, bd_ref,
                    r_out, k_out, v_out, kkn_out, b_out, lw_out, g_out, bonus_out):
    first = pl.program_id(0) == 0

    def shift(p_ref, prev_ref, mix_ref):
        p = p_ref[...]
        last = jnp.where(first, 0.0, prev_ref[SUBLANES - 1:SUBLANES, :])
        row = lax.broadcasted_iota(I32, p.shape, 0)
        prev = jnp.where(row == 0, last, pltpu.roll(p, 1, 0))
        return p + (prev - p) * mix_ref[...]

    r = shift(r_ref, rp_ref, mr_ref)
    k = shift(k_ref, kp_ref, mk_ref)
    v = shift(v_ref, vp_ref, mv_ref)
    zwa = shift(wa_ref, wap_ref, mwa_ref)
    zg = shift(zg_ref, zgp_ref, mg_ref)

    nd = -(w0_ref[...] + _dot3(jnp.tanh(zwa), wd_ref[...]))
    softplus = jnp.maximum(nd, 0.0) + jnp.log1p(jnp.exp(-jnp.abs(nd)))
    lw = -jnp.exp(-softplus - 0.5)
    a = jax.nn.sigmoid(a0_ref[...] + _dot3(zwa, wi_ref[...]))
    g = _dot3(jax.nn.sigmoid(zg), wg_ref[...])

    bd = bd_ref[...]

    def head_sum(x):
        hi, lo = _split2(x)
        return _mm(hi, bd) + _mm(lo, bd)

    kk = k * kk_ref[...]
    kkn = kk / jnp.maximum(jnp.sqrt(head_sum(kk * kk)), 1e-12)
    k4 = k * (1.0 + (a - 1.0) * ka_ref[...])
    r_out[...] = r
    k_out[...] = k4
    v_out[...] = v
    kkn_out[...] = kkn
    b_out[...] = kkn * a
    lw_out[...] = lw
    g_out[...] = g
    bonus_out[...] = head_sum(r * k4 * rk_ref[...]) * v


def _rw_prep(pm, pp, mixes, w0, wd, a0, wi, wg, k_k, k_a, r_k, bd, *, tm):
    T = pm.shape[0]
    W = RW_WIDTH
    pb = tm // SUBLANES

    def cur(width, col):
        return pl.BlockSpec((tm, width), lambda i: (i, col))

    def prev(width, col):
        return pl.BlockSpec((SUBLANES, width), lambda i: (jnp.maximum(i * pb - 1, 0), col))

    def full(shape):
        return pl.BlockSpec(shape, lambda i: (0,) * len(shape))

    row = lambda a: a.reshape(1, -1)
    out_sd = jax.ShapeDtypeStruct((T, W), F32)
    return pl.pallas_call(
        _rw_prep_kernel,
        grid=(T // tm,),
        in_specs=[cur(W, 4), cur(W, 5), cur(W, 6), cur(128, 6), cur(256, 2),
                  prev(W, 4), prev(W, 5), prev(W, 6), prev(128, 6), prev(256, 2),
                  full((1, W)), full((1, W)), full((1, W)), full((1, 128)), full((1, 256)),
                  full((1, W)), full((128, W)), full((1, W)), full((128, W)), full((256, W)),
                  full((1, W)), full((1, W)), full((1, W)), full((W, W))],
        out_specs=[pl.BlockSpec((tm, W), lambda i: (i, 0))] * 8,
        out_shape=[out_sd] * 8,
        compiler_params=_cp(("parallel",)),
        name="rw_prep",
    )(pm, pm, pm, pp, pp, pm, pm, pm, pp, pp,
      *[row(m) for m in mixes], row(w0), wd, row(a0), wi, wg, row(k_k), row(k_a), row(r_k), bd)


def _rw_chunk_kernel(r_ref, k_ref, v_ref, kkn_ref, b_ref, lw_ref, p_out, q_out, rp_out, y0_out, *, G):
    C = RW_CHUNK
    row = lax.broadcasted_iota(I32, (C, C), 0)
    col = lax.broadcasted_iota(I32, (C, C), 1)
    incl = row >= col
    strict = row > col
    eye = row == col
    same_head = (row // 64) == (col // 64)
    tri = jnp.where(incl, 1.0, 0.0).astype(BF16)
    lane = lax.broadcasted_iota(I32, (C, LANES), 1)
    eye_f = jnp.where(eye, 1.0, 0.0)

    def body(c, carry):
        sl = pl.ds(pl.multiple_of(c * C, C), C)
        r, k4, v, kkn, b, lw = (ref[sl, :] for ref in (r_ref, k_ref, v_ref, kkn_ref, b_ref, lw_ref))
        h1 = lw.astype(BF16)
        r1 = lw - h1.astype(F32)
        h2 = r1.astype(BF16)
        h3 = (r1 - h2.astype(F32)).astype(BF16)
        L = _mm(tri, h1) + (_mm(tri, h2) + _mm(tri, h3))
        Ltot = L[C - 1:C, :]
        Lr = L - L[C // 2 - 1:C // 2, :]
        epos = jnp.exp(Lr)
        eneg = jnp.exp(-Lr)
        rhat = r * epos
        ahat = -kkn * jnp.exp(Lr - lw)
        kt_s = _split2(k4 * eneg)
        bt_s = _split2(b * eneg)
        rabs = r * jnp.exp(L)
        aabs = -kkn * jnp.exp(L - lw)
        tail = jnp.exp(Ltot - L)
        kbar = k4 * tail
        bb_s = _split2(b * tail)
        gam = jnp.exp(Ltot)
        v_s = _split2(v)

        y0 = jnp.zeros((C, LANES), F32)
        u0 = jnp.zeros((C, LANES), F32)
        rp = jnp.zeros((C, LANES), F32)
        ah = jnp.zeros((C, LANES), F32)
        for hh in range(2):
            m = (lane >= 64) if hh else (lane < 64)
            am_s = _split2(jnp.where(m, ahat, 0.0))
            rm_s = _split2(jnp.where(m, rhat, 0.0))
            a_ab = jnp.where(strict, _d3(am_s, bt_s, _mm_nt), 0.0)
            a_ak = jnp.where(strict, _d3(am_s, kt_s, _mm_nt), 0.0)
            m_rb = jnp.where(incl, _d3(rm_s, bt_s, _mm_nt), 0.0)
            m_rk = jnp.where(incl, _d3(rm_s, kt_s, _mm_nt), 0.0)
            n = a_ab
            x = eye_f + n
            for _ in range(6):
                n = _dot3(n, n)
                x = x + _dot3(x, n)
            x_s = _split2(x)
            akv = _d3(_split2(a_ak), v_s)
            ah_h = _d3(x_s, _split2(jnp.where(m, aabs, 0.0)))
            u0_h = _d3(x_s, _split2(akv))
            mrb_s = _split2(m_rb)
            y0_h = _d3(_split2(m_rk), v_s) + _d3(mrb_s, _split2(u0_h))
            rp_h = jnp.where(m, rabs, 0.0) + _d3(mrb_s, _split2(ah_h))
            y0 = jnp.where(m, y0_h, y0)
            u0 = jnp.where(m, u0_h, u0)
            rp = rp + rp_h
            ah = ah + ah_h
        p_mat = jnp.where(eye, gam, 0.0) + jnp.where(same_head, _d3(bb_s, _split2(ah), _mm_tn), 0.0)
        q_mat = jnp.where(same_head, _d3(_split2(kbar), v_s, _mm_tn) + _d3(bb_s, _split2(u0), _mm_tn), 0.0)
        p_out[0, c] = p_mat
        q_out[0, c] = q_mat
        rp_out[sl, :] = rp
        y0_out[sl, :] = y0
        return carry

    lax.fori_loop(0, G, body, 0)


def _rw_chunk(r, k4, v, kkn, b, lw, *, G):
    T = r.shape[0]
    C = RW_CHUNK
    nch = T // C
    rows = G * C
    cur = pl.BlockSpec((rows, LANES), lambda p, g: (g, p))
    mat = pl.BlockSpec((1, G, C, C), lambda p, g: (p, g, 0, 0))
    return pl.pallas_call(
        functools.partial(_rw_chunk_kernel, G=G),
        grid=(RW_WIDTH // LANES, nch // G),
        in_specs=[cur] * 6,
        out_specs=[mat, mat, cur, cur],
        out_shape=[jax.ShapeDtypeStruct((RW_WIDTH // LANES, nch, C, C), F32)] * 2
        + [jax.ShapeDtypeStruct((T, RW_WIDTH), F32)] * 2,
        compiler_params=_cp(("parallel", "parallel")),
        name="rw_chunk",
    )(r, k4, v, kkn, b, lw)


def _rw_scan_kernel(p_ref, q_ref, rp_ref, y0_ref, y_out, s_scr):
    @pl.when(pl.program_id(0) == 0)
    def _():
        s_scr[...] = jnp.zeros_like(s_scr)

    for p in range(RW_WIDTH // LANES):
        ls = slice(p * LANES, (p + 1) * LANES)
        s0_s = _split2(s_scr[p])
        y_out[:, ls] = _d3(_split2(rp_ref[:, ls]), s0_s) + y0_ref[:, ls]
        s_scr[p] = _d3(_split2(p_ref[p, 0]), s0_s) + q_ref[p, 0]


def _rw_scan(p_mat, q_mat, rp, y0):
    T = rp.shape[0]
    C = RW_CHUNK
    npair = RW_WIDTH // LANES
    mat = pl.BlockSpec((npair, 1, C, C), lambda c: (0, c, 0, 0))
    cur = pl.BlockSpec((C, RW_WIDTH), lambda c: (c, 0))
    return pl.pallas_call(
        _rw_scan_kernel,
        grid=(T // C,),
        in_specs=[mat, mat, cur, cur],
        out_specs=cur,
        out_shape=jax.ShapeDtypeStruct((T, RW_WIDTH), F32),
        scratch_shapes=[pltpu.VMEM((npair, C, C), F32)],
        compiler_params=_cp(("arbitrary",)),
        name="rw_scan",
    )(p_mat, q_mat, rp, y0)


def _dsa_prep_kernel(q_ref, ckv_ref, g_ref, wuk_ref, c_out, ql_out):
    ckv = ckv_ref[...]
    c = ckv * lax.rsqrt(jnp.mean(ckv * ckv, axis=-1, keepdims=True) + EPS) * g_ref[...]
    c_out[...] = c.astype(BF16)
    for p in range(SA_HEADS // 2):
        qp = q_ref[:, p * LANES:(p + 1) * LANES].astype(BF16)
        ql = _mm(qp, wuk_ref[p]) * SA_SCALE
        ql_out[2 * p] = ql[:, :KV_RANK].astype(BF16)
        ql_out[2 * p + 1] = ql[:, KV_RANK:].astype(BF16)


def _dsa_prep(pm, kv_norm_g, wuk2, *, tm):
    T = pm.shape[0]
    return pl.pallas_call(
        _dsa_prep_kernel,
        grid=(T // tm,),
        in_specs=[pl.BlockSpec((tm, 1024), lambda i: (i, 7)),
                  pl.BlockSpec((tm, KV_RANK), lambda i: (i, 32)),
                  pl.BlockSpec((1, KV_RANK), lambda i: (0, 0)),
                  pl.BlockSpec((SA_HEADS // 2, LANES, 2 * KV_RANK), lambda i: (0, 0, 0))],
        out_specs=[pl.BlockSpec((tm, KV_RANK), lambda i: (i, 0)),
                   pl.BlockSpec((SA_HEADS, tm, KV_RANK), lambda i: (0, i, 0))],
        out_shape=[jax.ShapeDtypeStruct((T, KV_RANK), BF16),
                   jax.ShapeDtypeStruct((SA_HEADS, T, KV_RANK), BF16)],
        compiler_params=_cp(("parallel",)),
        name="dsa_prep",
    )(pm, pm, kv_norm_g.reshape(1, KV_RANK), wuk2)


def _bucket_bounds():
    nb = REL_BUCKETS // 2
    max_exact = nb // 2
    n = np.arange(1, 2 * REL_MAX_DIST, dtype=np.int64)
    large = max_exact + (np.log(n.astype(np.float32) / max_exact)
                         / math.log(REL_MAX_DIST / max_exact) * (nb - max_exact)).astype(np.int32)
    large = np.minimum(large, nb - 1)
    bounds = [int(n[np.argmax(large >= b)]) for b in range(max_exact + 1, nb)]
    assert all(b2 > b1 for b1, b2 in zip(bounds, bounds[1:]))
    return bounds


_BUCKET_BOUNDS = _bucket_bounds()


def _rel_bias_kernel(rb_ref, bt_out):
    o = pl.program_id(0)
    tl = lax.broadcasted_iota(I32, (Q_BLK, K_TILE), 0)
    sl = lax.broadcasted_iota(I32, (Q_BLK, K_TILE), 1)
    rel = sl - tl - o * Q_BLK
    nb = REL_BUCKETS // 2
    max_exact = nb // 2
    n = jnp.abs(rel)
    large = jnp.full(rel.shape, max_exact, I32)
    for bound in _BUCKET_BOUNDS:
        large = large + jnp.where(n >= bound, 1, 0)
    bucket = jnp.where(rel > 0, nb, 0) + jnp.where(n < max_exact, n, large)
    for h in range(SA_HEADS):
        acc = jnp.zeros(rel.shape, F32)
        for bkt in range(REL_BUCKETS):
            acc = jnp.where(bucket == bkt, rb_ref[bkt, h], acc)
        bt_out[0, h] = acc


def _rel_bias_tiles(rel_bias):
    return pl.pallas_call(
        _rel_bias_kernel,
        grid=(N_BIAS_TILES,),
        in_specs=[pl.BlockSpec(memory_space=pltpu.SMEM)],
        out_specs=pl.BlockSpec((1, SA_HEADS, Q_BLK, K_TILE), lambda o: (o, 0, 0, 0)),
        out_shape=jax.ShapeDtypeStruct((N_BIAS_TILES, SA_HEADS, Q_BLK, K_TILE), F32),
        compiler_params=_cp(("parallel",)),
        name="rel_bias",
    )(rel_bias)


def _dsa_kernel(iq_ref, iw_ref, ql_ref, ikh_ref, ikl_ref, c_ref, bt_ref, wuv_ref, o_out,
                key_scr, wb_scr, iqh_scr, iql_scr, m_scr, l_scr, acc_scr, *, topk):
    i = pl.program_id(0)
    t0 = i * Q_BLK
    n_tiles = i // 2 + 1
    H = SA_HEADS
    lane_q = lax.broadcasted_iota(I32, (Q_BLK, LANES), 1)

    for h in range(IDX_HEADS):
        pair = iq_ref[:, (h // 2) * LANES:(h // 2 + 1) * LANES]
        keep = (lane_q >= 64) if h % 2 else (lane_q < 64)
        hi, lo = _split2(jnp.where(keep, pair, 0.0))
        iqh_scr[h] = hi
        iql_scr[h] = lo
        wb_scr[h] = jnp.broadcast_to(iw_ref[:, 64 + h:65 + h] * IDX_SCALE, (Q_BLK, K_TILE))

    row_q = lax.broadcasted_iota(I32, (Q_BLK, K_TILE), 0)
    col_k = lax.broadcasted_iota(I32, (Q_BLK, K_TILE), 1)
    limit = t0 + (row_q // 64 + 1) * 64

    def score_tile(j, carry):
        s0 = pl.multiple_of(j * K_TILE, K_TILE)
        kh = ikh_ref[:, pl.ds(s0, K_TILE)]
        kl = ikl_ref[:, pl.ds(s0, K_TILE)]
        sc = jnp.zeros((Q_BLK, K_TILE), F32)
        for h in range(IDX_HEADS):
            d = _mm(iqh_scr[h], kh) + (_mm(iqh_scr[h], kl) + _mm(iql_scr[h], kh))
            sc = sc + jnp.maximum(d, 0.0) * wb_scr[h]
        sc = sc + 0.0
        bits = pltpu.bitcast(sc, I32)
        key = bits ^ ((bits >> 31) & 0x7FFFFFFF)
        key = jnp.where(s0 + col_k < limit, key, INT_MIN)
        key_scr[:, pl.ds(s0, K_TILE)] = key
        return carry

    lax.fori_loop(0, n_tiles, score_tile, 0)

    def count(pred):
        def tile(j, acc):
            s0 = pl.multiple_of(j * K_TILE, K_TILE)
            hit = jnp.where(pred(key_scr[:, pl.ds(s0, K_TILE)]), 1.0, 0.0)
            return acc + hit[:, :LANES] + hit[:, LANES:]
        acc = lax.fori_loop(0, n_tiles, tile, jnp.zeros((Q_BLK, LANES), F32))
        return jnp.sum(acc, axis=1, keepdims=True)

    def bit_step(it, cur):
        bit = lax.shift_left(jnp.int32(1), 31 - it)
        cand = (cur | bit) ^ INT_MIN
        cnt = count(lambda kt: kt >= cand)
        return jnp.where(cnt >= topk, cur | bit, cur)

    cur = lax.fori_loop(0, 32, bit_step, jnp.zeros((Q_BLK, 1), I32))
    theta = cur ^ INT_MIN
    need = topk - count(lambda kt: kt > theta)

    ur = lax.broadcasted_iota(I32, (K_TILE, K_TILE), 0)
    uc = lax.broadcasted_iota(I32, (K_TILE, K_TILE), 1)
    upper = jnp.where(ur <= uc, 1.0, 0.0).astype(BF16)

    def mask_tile(j, carry):
        s0 = pl.multiple_of(j * K_TILE, K_TILE)
        kt = key_scr[:, pl.ds(s0, K_TILE)]
        eq = kt == theta
        eq_f = jnp.where(eq, 1.0, 0.0)
        prefix = carry + _mm(eq_f.astype(BF16), upper)
        take = (kt > theta) | (eq & (prefix <= need))
        take = take & (s0 + col_k < limit)
        key_scr[:, pl.ds(s0, K_TILE)] = pltpu.bitcast(jnp.where(take, 0.0, NEG_BIG), I32)
        return carry + jnp.sum(eq_f, axis=1, keepdims=True)

    lax.fori_loop(0, n_tiles, mask_tile, jnp.zeros((Q_BLK, 1), F32))

    m_scr[...] = jnp.full(m_scr.shape, NEG_BIG, F32)
    l_scr[...] = jnp.zeros(l_scr.shape, F32)
    acc_scr[...] = jnp.zeros(acc_scr.shape, F32)
    ql = ql_ref[...].reshape(H * Q_BLK, KV_RANK)

    def attn_tile(j, carry):
        s0 = pl.multiple_of(j * K_TILE, K_TILE)
        c_t = c_ref[pl.ds(s0, K_TILE), :]
        logits = _mm_nt(ql, c_t).reshape(H, Q_BLK, K_TILE)
        addm = pltpu.bitcast(key_scr[:, pl.ds(s0, K_TILE)], F32)
        off = jnp.minimum(i - 2 * j, N_BIAS_TILES - 1)
        logits = logits + bt_ref[off] + addm[None]
        m_prev = m_scr[...]
        m_new = jnp.maximum(m_prev, jnp.max(logits, axis=-1, keepdims=True))
        alpha = jnp.exp(m_prev - m_new)
        p = jnp.exp(logits - m_new)
        l_scr[...] = alpha * l_scr[...] + jnp.sum(p, axis=-1, keepdims=True)
        pv = _mm(p.astype(BF16).reshape(H * Q_BLK, K_TILE), c_t).reshape(H, Q_BLK, KV_RANK)
        acc_scr[...] = alpha * acc_scr[...] + pv
        m_scr[...] = m_new
        return carry

    lax.fori_loop(0, n_tiles, attn_tile, 0)

    o_lat = (acc_scr[...] / l_scr[...]).astype(BF16)
    for p in range(H // 2):
        pair = jnp.concatenate([o_lat[2 * p], o_lat[2 * p + 1]], axis=1)
        o_out[:, p * LANES:(p + 1) * LANES] = _mm(pair, wuv_ref[p])


def _dsa(pp, ql, ikh, ikl, c, bt, wuv2, *, topk):
    T = pp.shape[0]
    H = SA_HEADS
    whole = pl.BlockSpec(memory_space=pltpu.VMEM)
    return pl.pallas_call(
        functools.partial(_dsa_kernel, topk=topk),
        grid=(T // Q_BLK,),
        in_specs=[pl.BlockSpec((Q_BLK, 512), lambda i: (i, 0)),
                  pl.BlockSpec((Q_BLK, LANES), lambda i: (i, 7)),
                  pl.BlockSpec((H, Q_BLK, KV_RANK), lambda i: (0, i, 0)),
                  whole, whole, whole, whole, whole],
        out_specs=pl.BlockSpec((Q_BLK, 1024), lambda i: (i, 0)),
        out_shape=jax.ShapeDtypeStruct((T, 1024), F32),
        scratch_shapes=[pltpu.VMEM((Q_BLK, T), I32),
                        pltpu.VMEM((IDX_HEADS, Q_BLK, K_TILE), F32),
                        pltpu.VMEM((IDX_HEADS, Q_BLK, LANES), BF16),
                        pltpu.VMEM((IDX_HEADS, Q_BLK, LANES), BF16),
                        pltpu.VMEM((H, Q_BLK, 1), F32),
                        pltpu.VMEM((H, Q_BLK, 1), F32),
                        pltpu.VMEM((H, Q_BLK, KV_RANK), F32)],
        compiler_params=_cp(("parallel",), 56),
        name="dsa",
    )(pp, pp, ql, ikh, ikl, c, bt, wuv2)


def _merge_kernel(y_ref, bonus_ref, g_ref, o_ref, ga_ref, gb_ref, x_ref, lnw_ref, lnb_ref, bd_ref,
                  wa_ref, wb_ref, wout_ref, h_out, m_scr):
    @pl.when(pl.program_id(1) == 0)
    def _():
        bd = bd_ref[...]

        def head_mean(x):
            hi, lo = _split2(x)
            return (_mm(hi, bd) + _mm(lo, bd)) * (1.0 / 64)

        y = y_ref[...]
        yc = y - head_mean(y)
        yn = yc * lax.rsqrt(head_mean(yc * yc) + RW_GN_EPS) * lnw_ref[...] + lnb_ref[...]
        ya = (yn + bonus_ref[...]) * g_ref[...]
        pa = _mm(ya.astype(BF16), wa_ref[...])
        pb = _mm(o_ref[...].astype(BF16), wb_ref[...])
        m = jax.nn.sigmoid(ga_ref[...]) * pa + jax.nn.sigmoid(gb_ref[...]) * pb
        m_scr[...] = m.astype(BF16)

    h_out[...] = x_ref[...] + _mm(m_scr[...], wout_ref[...])


def _merge(y, bonus, g, o, pm, x, ln_w, ln_b, bd, wa, wb, wout, *, tm, tn):
    T, D = x.shape
    W = RW_WIDTH
    row = lambda a: a.reshape(1, -1)
    act = pl.BlockSpec((tm, W), lambda i, j: (i, 0))
    full = lambda shape: pl.BlockSpec(shape, lambda i, j: (0,) * len(shape))
    return pl.pallas_call(
        _merge_kernel,
        grid=(T // tm, D // tn),
        in_specs=[act, act, act, act,
                  pl.BlockSpec((tm, D), lambda i, j: (i, 0)),
                  pl.BlockSpec((tm, D), lambda i, j: (i, 1)),
                  pl.BlockSpec((tm, tn), lambda i, j: (i, j)),
                  full((1, W)), full((1, W)), full((W, W)), full((W, D)), full((W, D)),
                  pl.BlockSpec((D, tn), lambda i, j: (0, j))],
        out_specs=pl.BlockSpec((tm, tn), lambda i, j: (i, j)),
        out_shape=jax.ShapeDtypeStruct((T, D), F32),
        scratch_shapes=[pltpu.VMEM((tm, D), BF16)],
        compiler_params=_cp(("parallel", "arbitrary"), 56),
        name="merge",
    )(y, bonus, g, o, pm, pm, x, row(ln_w), row(ln_b), bd, wa, wb, wout)


def _sort16_pairs():
    n, pairs, p = 16, [], 1
    while p < n:
        k = p
        while k >= 1:
            for j in range(k % p, n - k, 2 * k):
                for i in range(min(k, n - j - k)):
                    if (i + j) // (2 * p) == (i + j + k) // (2 * p):
                        pairs.append((i + j, i + j + k))
            k //= 2
        p *= 2
    return pairs


_SORT16 = _sort16_pairs()
PEER_MARGIN = 1.0 - 2.0 ** -20


def _exchange(a, i, j):
    a[i], a[j] = jnp.maximum(a[i], a[j]), jnp.minimum(a[i], a[j])


def _bitonic_sort16(a):
    for dist in (8, 4, 2, 1):
        for i in range(PEER_TOPK):
            if not i & dist:
                _exchange(a, i, i + dist)
    return a


def _top16_merge(x, y):
    return _bitonic_sort16([jnp.maximum(x[i], y[PEER_TOPK - 1 - i]) for i in range(PEER_TOPK)])


def _peer_select_kernel(qp_ref, keys_ref, et_out, tau_out):
    tm = qp_ref.shape[1]
    sub = lax.broadcasted_iota(I32, (SUBLANES, tm), 0)
    zero = jnp.zeros((SUBLANES, tm), F32)

    def per_head(h, packed):
        packed = list(packed)
        for c in range(2):
            hc = 2 * h + c
            s_t = _dot3(keys_ref[hc], qp_ref[hc], _mm_nt)
            e = jnp.exp(s_t - jnp.max(s_t, axis=0, keepdims=True))
            et_out[hc] = e
            a = [e[g * SUBLANES:(g + 1) * SUBLANES, :] for g in range(PEER_KEYS // SUBLANES)]
            for i, j in _SORT16:
                _exchange(a, i, j)
            for shift in (4, 2, 1):
                a = _top16_merge(a, [pltpu.roll(t, shift, 0) for t in a])
            for i in range(PEER_TOPK):
                packed[c * PEER_TOPK + i] = jnp.where(sub == h, a[i], packed[c * PEER_TOPK + i])
        return tuple(packed)

    packed = lax.fori_loop(0, PEER_HEADS, per_head, (zero,) * (2 * PEER_TOPK))
    top_a, top_b = packed[:PEER_TOPK], packed[PEER_TOPK:]

    def run(i, n):
        return [top_a[i] * top_b[j] for j in range(n)]

    g1 = run(0, 16)
    g2 = _bitonic_sort16(run(1, 8) + [top_a[i] * top_b[0] for i in range(PEER_TOPK - 1, 7, -1)])
    g3 = run(2, 5) + run(3, 4) + run(4, 3) + run(5, 2) + run(6, 2)
    for i, j in _SORT16:
        _exchange(g3, i, j)
    top = _top16_merge(_top16_merge(_top16_merge(g1, g2), g3), run(7, 2) + [zero] * 14)
    z = top[0]
    for t in top[1:]:
        z = z + t
    zinv = 1.0 / z
    tau_out[...] = top[PEER_TOPK - 1] * PEER_MARGIN * zinv
    for h in range(PEER_HEADS):
        et_out[2 * h] = et_out[2 * h] * zinv[h:h + 1, :]


def _peer_select(qp, keys, *, tm):
    T = qp.shape[1]
    return pl.pallas_call(
        _peer_select_kernel,
        grid=(T // tm,),
        in_specs=[pl.BlockSpec((2 * PEER_HEADS, tm, LANES), lambda i: (0, i, 0)),
                  pl.BlockSpec((2 * PEER_HEADS, PEER_KEYS, LANES), lambda i: (0, 0, 0))],
        out_specs=[pl.BlockSpec((2 * PEER_HEADS, PEER_KEYS, tm), lambda i: (0, 0, i)),
                   pl.BlockSpec((SUBLANES, tm), lambda i: (0, i))],
        out_shape=[jax.ShapeDtypeStruct((2 * PEER_HEADS, PEER_KEYS, T), F32),
                   jax.ShapeDtypeStruct((PEER_HEADS, T), F32)],
        compiler_params=_cp(("parallel",)),
        name="peer_select",
    )(qp, keys)


def _peer_kernel(h_ref, gf_ref, gout_ref, u_ref, vt_ref, et_ref, tau_ref, out_ref, hn_scr, acc_scr, *, e_tile):
    j = pl.program_id(1)
    tm = h_ref.shape[0]

    @pl.when(j == 0)
    def _():
        h = h_ref[...]
        hn = h * lax.rsqrt(jnp.mean(h * h, axis=-1, keepdims=True) + EPS) * gf_ref[...]
        hn_scr[...] = hn.astype(BF16)
        acc_scr[...] = jnp.zeros_like(acc_scr)

    pre = _mm_nt(u_ref[...], hn_scr[...])
    act = 0.5 * pre * (1.0 + lax.erf(pre * (2.0 ** -0.5)))
    parts = []
    for ai in range(e_tile // PEER_KEYS):
        a = j * (e_tile // PEER_KEYS) + ai
        gate = jnp.zeros((PEER_KEYS, tm), F32)
        for h in range(PEER_HEADS):
            w = et_ref[2 * h + 1] * et_ref[2 * h, pl.ds(a, 1), :]
            gate = gate + jnp.where(w >= tau_ref[h:h + 1, :], w, 0.0)
        parts.append((gate * act[ai * PEER_KEYS:(ai + 1) * PEER_KEYS, :]).astype(BF16))
    acc_scr[...] += _mm(vt_ref[...], jnp.concatenate(parts, axis=0))

    @pl.when(j == pl.num_programs(1) - 1)
    def _():
        h3 = h_ref[...] + acc_scr[...].T
        out_ref[...] = h3 * lax.rsqrt(jnp.mean(h3 * h3, axis=-1, keepdims=True) + EPS) * gout_ref[...]


def _peer(h1, g_ffn, g_out, u, vt, et, tau, *, tm, e_tile):
    T, D = h1.shape
    E = u.shape[0]
    return pl.pallas_call(
        functools.partial(_peer_kernel, e_tile=e_tile),
        grid=(T // tm, E // e_tile),
        in_specs=[pl.BlockSpec((tm, D), lambda i, j: (i, 0)),
                  pl.BlockSpec((1, D), lambda i, j: (0, 0)),
                  pl.BlockSpec((1, D), lambda i, j: (0, 0)),
                  pl.BlockSpec((e_tile, D), lambda i, j: (j, 0)),
                  pl.BlockSpec((D, e_tile), lambda i, j: (0, j)),
                  pl.BlockSpec((2 * PEER_HEADS, PEER_KEYS, tm), lambda i, j: (0, 0, i)),
                  pl.BlockSpec((PEER_HEADS, tm), lambda i, j: (0, i))],
        out_specs=pl.BlockSpec((tm, D), lambda i, j: (i, 0)),
        out_shape=jax.ShapeDtypeStruct((T, D), F32),
        scratch_shapes=[pltpu.VMEM((tm, D), BF16), pltpu.VMEM((D, tm), F32)],
        compiler_params=_cp(("parallel", "arbitrary"), 56),
        name="peer",
    )(h1, g_ffn.reshape(1, D), g_out.reshape(1, D), u, vt, et, tau)


def _pad_cols(a, n):
    return jnp.pad(a, ((0, 0), (0, n - a.shape[1])))


def _block_diag2(a, b):
    za = jnp.zeros((a.shape[0], b.shape[1]), a.dtype)
    zb = jnp.zeros((b.shape[0], a.shape[1]), a.dtype)
    return jnp.concatenate([jnp.concatenate([a, za], axis=1), jnp.concatenate([zb, b], axis=1)], axis=0)


def kernel(x, norm_mix_g, w_in, shift_mix, w0, w_decay_up, a0, w_icl_up, w_gate_up, k_k, k_a, r_k, ln_x_w, ln_x_b,
           kv_norm_g, w_uk, w_uv, rel_bias, w_branch_rwkv, w_branch_dsa, w_out, norm_ffn_g, w_peer_query,
           peer_sub_keys, peer_u, peer_v, norm_final_g):
    B, T, D = x.shape
    assert B == 1 and D == D_MODEL and T % (2 * K_TILE) == 0 and norm_mix_g.shape[0] == 1
    topk = min(TOPK_MAX, T // 4)
    x2 = x[0]
    w = w_in[0]
    sm = shift_mix[0]
    o_zw, o_za, o_zg, o_q, o_ckv, o_iq, o_ik, o_iw, o_g = 3072, 3136, 3200, 3360, 4384, 4640, 5152, 5216, 5224
    w_main = jnp.concatenate([w[:, o_g:o_g + 4096], w[:, :3072], w[:, o_q:o_q + 1024], w[:, o_ckv:o_ckv + 256]],
                             axis=1).astype(BF16)
    w_prec = jnp.concatenate([w[:, o_iq:o_iq + 512], _pad_cols(w[:, o_zg:o_zg + 160], 256), w[:, o_zw:o_zw + 128],
                              _pad_cols(w[:, o_ik:o_ik + 72], 128)], axis=1)
    tm = min(512, T)
    pm = _proj(x2, norm_mix_g[0], w_main, precise=False, tm=tm, tn=768, name="proj_main")
    pp = _proj(x2, norm_mix_g[0], w_prec, precise=True, tm=tm, tn=512, name="proj_precise")

    head_id = jnp.arange(RW_WIDTH) // 64
    bd = (head_id[:, None] == head_id[None, :]).astype(BF16)
    mixes = [sm[:1024], sm[1024:2048], sm[2048:3072], sm[o_zw:o_zw + 128],
             jnp.pad(sm[o_zg:o_zg + 160], (0, 96))]
    wd = jnp.pad(w_decay_up[0], ((0, 64), (0, 0)))
    wi = jnp.pad(w_icl_up[0], ((64, 0), (0, 0)))
    wg = jnp.pad(w_gate_up[0], ((0, 96), (0, 0)))
    r, k4, v, kkn, b, lw, g, bonus = _rw_prep(pm, pp, mixes, w0[0], wd, a0[0], wi, wg, k_k[0], k_a[0],
                                              r_k[0].reshape(-1), bd, tm=min(256, T))
    nch = T // RW_CHUNK
    p_mat, q_mat, rp, y0 = _rw_chunk(r, k4, v, kkn, b, lw, G=min(8, nch))
    y = _rw_scan(p_mat, q_mat, rp, y0)

    wuk_t = jnp.transpose(w_uk[0], (0, 2, 1))
    wuk2 = jnp.stack([_block_diag2(wuk_t[2 * p], wuk_t[2 * p + 1]) for p in range(SA_HEADS // 2)]).astype(BF16)
    wuv2 = jnp.stack([_block_diag2(w_uv[0][2 * p], w_uv[0][2 * p + 1]) for p in range(SA_HEADS // 2)]).astype(BF16)
    c, ql = _dsa_prep(pm, kv_norm_g[0], wuk2, tm=min(512, T))
    ik_t = jnp.transpose(pp[:, 896:960])
    ik_t2 = jnp.concatenate([ik_t, ik_t], axis=0)
    ikh = ik_t2.astype(BF16)
    ikl = (ik_t2 - ikh.astype(F32)).astype(BF16)
    bt = _rel_bias_tiles(rel_bias)
    o_dsa = _dsa(pp, ql, ikh, ikl, c, bt, wuv2, topk=topk)

    h1 = _merge(y, bonus, g, o_dsa, pm, x2, ln_x_w[0], ln_x_b[0], bd,
                w_branch_rwkv[0].astype(BF16), w_branch_dsa[0].astype(BF16), w_out[0].astype(BF16),
                tm=min(256, T), tn=512)

    qp = _proj(h1, norm_ffn_g[0], w_peer_query[0], precise=True, tm=tm, tn=512, name="peer_query", lane_groups=True)
    keys = peer_sub_keys[0].reshape(2 * PEER_HEADS, PEER_KEYS, LANES)
    et, tau = _peer_select(qp, keys, tm=128)
    u = peer_u[0].astype(BF16)
    vt = jnp.transpose(peer_v[0]).astype(BF16)
    out = _peer(h1, norm_ffn_g[0], norm_final_g, u, vt, et, tau, tm=min(512, T), e_tile=512)
    return out[None]
```

```python
import functools
import math

import numpy as np
import jax
import jax.numpy as jnp
from jax import lax
from jax.experimental import pallas as pl
from jax.experimental.pallas import tpu as pltpu

F32, BF16, I32 = jnp.float32, jnp.bfloat16, jnp.int32

D_MODEL = 2048
EPS = 1e-6
RW_WIDTH = 1024
RW_GN_EPS = 64e-5
RW_CHUNK = 128
SA_HEADS = 16
SA_SCALE = 64 ** -0.5
KV_RANK = 256
IDX_HEADS = 8
IDX_SCALE = (8 * 64) ** -0.5
TOPK_MAX = 256
REL_BUCKETS = 32
REL_MAX_DIST = 1024
PEER_HEADS = 8
PEER_KEYS = 128
PEER_TOPK = 16
LANES = 128
SUBLANES = 8
NEG_BIG = -1e30
INT_MIN = -2 ** 31

Q_BLK = 128
K_TILE = 256
N_BIAS_TILES = 7
R_TILES = 4
LOG2E = math.log2(math.e)


def _cp(sem, vmem_mb=48):
    return pltpu.CompilerParams(dimension_semantics=sem, vmem_limit_bytes=vmem_mb * 1024 * 1024)


def _split2(x):
    hi = x.astype(BF16)
    lo = (x - hi.astype(F32)).astype(BF16)
    return hi, lo


def _mm(a, b):
    return jnp.dot(a, b, preferred_element_type=F32)


def _mm_nt(a, b):
    return lax.dot_general(a, b, (((1,), (1,)), ((), ())), preferred_element_type=F32)


def _mm_tn(a, b):
    return lax.dot_general(a, b, (((0,), (0,)), ((), ())), preferred_element_type=F32)


def _d3(a_s, b_s, mm=_mm):
    return mm(a_s[0], b_s[0]) + (mm(a_s[0], b_s[1]) + mm(a_s[1], b_s[0]))


def _dot3(a, b, mm=_mm):
    return _d3(_split2(a), _split2(b), mm)


def _proj_kernel(x_ref, g_ref, w_ref, o_ref, xh_ref, xl_ref, *, precise, lane_groups):
    @pl.when(pl.program_id(1) == 0)
    def _():
        x = x_ref[...]
        xn = x * lax.rsqrt(jnp.mean(x * x, axis=-1, keepdims=True) + EPS) * g_ref[...]
        hi = xn.astype(BF16)
        xh_ref[...] = hi
        if precise:
            xl_ref[...] = (xn - hi.astype(F32)).astype(BF16)

    if precise:
        wh, wl = _split2(w_ref[...])
        xh = xh_ref[...]
        res = _mm(xh, wh) + (_mm(xh, wl) + _mm(xl_ref[...], wh))
    else:
        res = _mm(xh_ref[...], w_ref[...])
    if lane_groups:
        for k in range(res.shape[1] // LANES):
            o_ref[k] = res[:, k * LANES:(k + 1) * LANES]
    else:
        o_ref[...] = res


def _proj(x, g, w, *, precise, tm, tn, name, lane_groups=False):
    T, D = x.shape
    N = w.shape[1]
    lo_rows = tm if precise else SUBLANES * 2
    if lane_groups:
        out_spec = pl.BlockSpec((tn // LANES, tm, LANES), lambda i, j: (j, i, 0))
        out_shape = jax.ShapeDtypeStruct((N // LANES, T, LANES), F32)
    else:
        out_spec = pl.BlockSpec((tm, tn), lambda i, j: (i, j))
        out_shape = jax.ShapeDtypeStruct((T, N), F32)
    return pl.pallas_call(
        functools.partial(_proj_kernel, precise=precise, lane_groups=lane_groups),
        grid=(T // tm, N // tn),
        in_specs=[pl.BlockSpec((tm, D), lambda i, j: (i, 0)),
                  pl.BlockSpec((1, D), lambda i, j: (0, 0)),
                  pl.BlockSpec((D, tn), lambda i, j: (0, j))],
        out_specs=out_spec,
        out_shape=out_shape,
        scratch_shapes=[pltpu.VMEM((tm, D), BF16), pltpu.VMEM((lo_rows, D), BF16)],
        compiler_params=_cp(("parallel", "arbitrary")),
        name=name,
    )(x, g.reshape(1, D), w)


def _rw_prep_kernel(r_ref, k_ref, v_ref, wa_ref, zg_ref, rp_ref, kp_ref, vp_ref, wap_ref, zgp_ref,
                    mr_ref, mk_ref, mv_ref, mwa_ref, mg_ref, w0_ref, wd_ref, a0_ref, wi_ref, wg_ref,
                    kk_ref, ka_ref, rk_ref, bd_ref,
                    r_out, k_out, v_out, kkn_out, b_out, lw_out, g_out, bonus_out):
    first = pl.program_id(0) == 0

    def shift(p_ref, prev_ref, mix_ref):
        p = p_ref[...]
        last = jnp.where(first, 0.0, prev_ref[SUBLANES - 1:SUBLANES, :])
        row = lax.broadcasted_iota(I32, p.shape, 0)
        prev = jnp.where(row == 0, last, pltpu.roll(p, 1, 0))
        return p + (prev - p) * mix_ref[...]

    r = shift(r_ref, rp_ref, mr_ref)
    k = shift(k_ref, kp_ref, mk_ref)
    v = shift(v_ref, vp_ref, mv_ref)
    zwa = shift(wa_ref, wap_ref, mwa_ref)
    zg = shift(zg_ref, zgp_ref, mg_ref)

    nd = -(w0_ref[...] + _dot3(jnp.tanh(zwa), wd_ref[...]))
    softplus = jnp.maximum(nd, 0.0) + jnp.log1p(jnp.exp(-jnp.abs(nd)))
    lw = -jnp.exp(-softplus - 0.5)
    a = jax.nn.sigmoid(a0_ref[...] + _dot3(zwa, wi_ref[...]))
    g = _dot3(jax.nn.sigmoid(zg), wg_ref[...])

    bd = bd_ref[...]

    def head_sum(x):
        hi, lo = _split2(x)
        return _mm(hi, bd) + _mm(lo, bd)

    kk = k * kk_ref[...]
    kkn = kk / jnp.maximum(jnp.sqrt(head_sum(kk * kk)), 1e-12)
    k4 = k * (1.0 + (a - 1.0) * ka_ref[...])
    r_out[...] = r
    k_out[...] = k4
    v_out[...] = v
    kkn_out[...] = kkn
    b_out[...] = kkn * a
    lw_out[...] = lw
    g_out[...] = g
    bonus_out[...] = head_sum(r * k4 * rk_ref[...]) * v


def _rw_prep(pm, pp, mixes, w0, wd, a0, wi, wg, k_k, k_a, r_k, bd, *, tm):
    T = pm.shape[0]
    W = RW_WIDTH
    pb = tm // SUBLANES

    def cur(width, col):
        return pl.BlockSpec((tm, width), lambda i: (i, col))

    def prev(width, col):
        return pl.BlockSpec((SUBLANES, width), lambda i: (jnp.maximum(i * pb - 1, 0), col))

    def full(shape):
        return pl.BlockSpec(shape, lambda i: (0,) * len(shape))

    row = lambda a: a.reshape(1, -1)
    out_sd = jax.ShapeDtypeStruct((T, W), F32)
    return pl.pallas_call(
        _rw_prep_kernel,
        grid=(T // tm,),
        in_specs=[cur(W, 4), cur(W, 5), cur(W, 6), cur(128, 6), cur(256, 2),
                  prev(W, 4), prev(W, 5), prev(W, 6), prev(128, 6), prev(256, 2),
                  full((1, W)), full((1, W)), full((1, W)), full((1, 128)), full((1, 256)),
                  full((1, W)), full((128, W)), full((1, W)), full((128, W)), full((256, W)),
                  full((1, W)), full((1, W)), full((1, W)), full((W, W))],
        out_specs=[pl.BlockSpec((tm, W), lambda i: (i, 0))] * 8,
        out_shape=[out_sd] * 8,
        compiler_params=_cp(("parallel",)),
        name="rw_prep",
    )(pm, pm, pm, pp, pp, pm, pm, pm, pp, pp,
      *[row(m) for m in mixes], row(w0), wd, row(a0), wi, wg, row(k_k), row(k_a), row(r_k), bd)


def _rw_chunk_kernel(r_ref, k_ref, v_ref, kkn_ref, b_ref, lw_ref, p_out, q_out, rp_out, y0_out, *, G):
    C = RW_CHUNK
    row = lax.broadcasted_iota(I32, (C, C), 0)
    col = lax.broadcasted_iota(I32, (C, C), 1)
    incl = row >= col
    strict = row > col
    eye = row == col
    same_head = (row // 64) == (col // 64)
    tri = jnp.where(incl, 1.0, 0.0).astype(BF16)
    lane = lax.broadcasted_iota(I32, (C, LANES), 1)
    eye_f = jnp.where(eye, 1.0, 0.0)

    def body(c, carry):
        sl = pl.ds(pl.multiple_of(c * C, C), C)
        r, k4, v, kkn, b, lw = (ref[sl, :] for ref in (r_ref, k_ref, v_ref, kkn_ref, b_ref, lw_ref))
        h1 = lw.astype(BF16)
        r1 = lw - h1.astype(F32)
        h2 = r1.astype(BF16)
        h3 = (r1 - h2.astype(F32)).astype(BF16)
        L = _mm(tri, h1) + (_mm(tri, h2) + _mm(tri, h3))
        Ltot = L[C - 1:C, :]
        Lr = L - L[C // 2 - 1:C // 2, :]
        epos = jnp.exp(Lr)
        eneg = jnp.exp(-Lr)
        rhat = r * epos
        ahat = -kkn * jnp.exp(Lr - lw)
        kt_s = _split2(k4 * eneg)
        bt_s = _split2(b * eneg)
        rabs = r * jnp.exp(L)
        aabs = -kkn * jnp.exp(L - lw)
        tail = jnp.exp(Ltot - L)
        kbar = k4 * tail
        bb_s = _split2(b * tail)
        gam = jnp.exp(Ltot)
        v_s = _split2(v)

        y0 = jnp.zeros((C, LANES), F32)
        u0 = jnp.zeros((C, LANES), F32)
        rp = jnp.zeros((C, LANES), F32)
        ah = jnp.zeros((C, LANES), F32)
        for hh in range(2):
            m = (lane >= 64) if hh else (lane < 64)
            am_s = _split2(jnp.where(m, ahat, 0.0))
            rm_s = _split2(jnp.where(m, rhat, 0.0))
            a_ab = jnp.where(strict, _d3(am_s, bt_s, _mm_nt), 0.0)
            a_ak = jnp.where(strict, _d3(am_s, kt_s, _mm_nt), 0.0)
            m_rb = jnp.where(incl, _d3(rm_s, bt_s, _mm_nt), 0.0)
            m_rk = jnp.where(incl, _d3(rm_s, kt_s, _mm_nt), 0.0)
            n = a_ab
            x = eye_f + n
            for _ in range(6):
                n = _dot3(n, n)
                x = x + _dot3(x, n)
            x_s = _split2(x)
            akv = _d3(_split2(a_ak), v_s)
            ah_h = _d3(x_s, _split2(jnp.where(m, aabs, 0.0)))
            u0_h = _d3(x_s, _split2(akv))
            mrb_s = _split2(m_rb)
            y0_h = _d3(_split2(m_rk), v_s) + _d3(mrb_s, _split2(u0_h))
            rp_h = jnp.where(m, rabs, 0.0) + _d3(mrb_s, _split2(ah_h))
            y0 = jnp.where(m, y0_h, y0)
            u0 = jnp.where(m, u0_h, u0)
            rp = rp + rp_h
            ah = ah + ah_h
        p_mat = jnp.where(eye, gam, 0.0) + jnp.where(same_head, _d3(bb_s, _split2(ah), _mm_tn), 0.0)
        q_mat = jnp.where(same_head, _d3(_split2(kbar), v_s, _mm_tn) + _d3(bb_s, _split2(u0), _mm_tn), 0.0)
        p_out[0, c] = p_mat
        q_out[0, c] = q_mat
        rp_out[sl, :] = rp
        y0_out[sl, :] = y0
        return carry

    lax.fori_loop(0, G, body, 0)


def _rw_chunk(r, k4, v, kkn, b, lw, *, G):
    T = r.shape[0]
    C = RW_CHUNK
    nch = T // C
    rows = G * C
    cur = pl.BlockSpec((rows, LANES), lambda p, g: (g, p))
    mat = pl.BlockSpec((1, G, C, C), lambda p, g: (p, g, 0, 0))
    return pl.pallas_call(
        functools.partial(_rw_chunk_kernel, G=G),
        grid=(RW_WIDTH // LANES, nch // G),
        in_specs=[cur] * 6,
        out_specs=[mat, mat, cur, cur],
        out_shape=[jax.ShapeDtypeStruct((RW_WIDTH // LANES, nch, C, C), F32)] * 2
        + [jax.ShapeDtypeStruct((T, RW_WIDTH), F32)] * 2,
        compiler_params=_cp(("parallel", "parallel")),
        name="rw_chunk",
    )(r, k4, v, kkn, b, lw)


def _rw_scan_kernel(p_ref, q_ref, rp_ref, y0_ref, y_out, s_scr):
    @pl.when(pl.program_id(0) == 0)
    def _():
        s_scr[...] = jnp.zeros_like(s_scr)

    for p in range(RW_WIDTH // LANES):
        ls = slice(p * LANES, (p + 1) * LANES)
        s0_s = _split2(s_scr[p])
        y_out[:, ls] = _d3(_split2(rp_ref[:, ls]), s0_s) + y0_ref[:, ls]
        s_scr[p] = _d3(_split2(p_ref[p, 0]), s0_s) + q_ref[p, 0]


def _rw_scan(p_mat, q_mat, rp, y0):
    T = rp.shape[0]
    C = RW_CHUNK
    npair = RW_WIDTH // LANES
    mat = pl.BlockSpec((npair, 1, C, C), lambda c: (0, c, 0, 0))
    cur = pl.BlockSpec((C, RW_WIDTH), lambda c: (c, 0))
    return pl.pallas_call(
        _rw_scan_kernel,
        grid=(T // C,),
        in_specs=[mat, mat, cur, cur],
        out_specs=cur,
        out_shape=jax.ShapeDtypeStruct((T, RW_WIDTH), F32),
        scratch_shapes=[pltpu.VMEM((npair, C, C), F32)],
        compiler_params=_cp(("arbitrary",)),
        name="rw_scan",
    )(p_mat, q_mat, rp, y0)


def _dsa_prep_kernel(q_ref, ckv_ref, g_ref, wuk_ref, c_out, ql_out):
    ckv = ckv_ref[...]
    c = ckv * lax.rsqrt(jnp.mean(ckv * ckv, axis=-1, keepdims=True) + EPS) * g_ref[...]
    c_out[...] = c.astype(BF16)
    for p in range(SA_HEADS // 2):
        qp = q_ref[:, p * LANES:(p + 1) * LANES].astype(BF16)
        ql = _mm(qp, wuk_ref[p]) * (SA_SCALE * LOG2E)
        ql_out[2 * p] = ql[:, :KV_RANK].astype(BF16)
        ql_out[2 * p + 1] = ql[:, KV_RANK:].astype(BF16)


def _dsa_prep(pm, kv_norm_g, wuk2, *, tm):
    T = pm.shape[0]
    return pl.pallas_call(
        _dsa_prep_kernel,
        grid=(T // tm,),
        in_specs=[pl.BlockSpec((tm, 1024), lambda i: (i, 7)),
                  pl.BlockSpec((tm, KV_RANK), lambda i: (i, 32)),
                  pl.BlockSpec((1, KV_RANK), lambda i: (0, 0)),
                  pl.BlockSpec((SA_HEADS // 2, LANES, 2 * KV_RANK), lambda i: (0, 0, 0))],
        out_specs=[pl.BlockSpec((tm, KV_RANK), lambda i: (i, 0)),
                   pl.BlockSpec((SA_HEADS, tm, KV_RANK), lambda i: (0, i, 0))],
        out_shape=[jax.ShapeDtypeStruct((T, KV_RANK), BF16),
                   jax.ShapeDtypeStruct((SA_HEADS, T, KV_RANK), BF16)],
        compiler_params=_cp(("parallel",)),
        name="dsa_prep",
    )(pm, pm, kv_norm_g.reshape(1, KV_RANK), wuk2)


def _bucket_bounds():
    nb = REL_BUCKETS // 2
    max_exact = nb // 2
    n = np.arange(1, 2 * REL_MAX_DIST, dtype=np.int64)
    large = max_exact + (np.log(n.astype(np.float32) / max_exact)
                         / math.log(REL_MAX_DIST / max_exact) * (nb - max_exact)).astype(np.int32)
    large = np.minimum(large, nb - 1)
    bounds = [int(n[np.argmax(large >= b)]) for b in range(max_exact + 1, nb)]
    assert all(b2 > b1 for b1, b2 in zip(bounds, bounds[1:]))
    return bounds


_BUCKET_BOUNDS = _bucket_bounds()


def _rel_bias_kernel(rb_ref, bt_out):
    o = pl.program_id(0)
    sl = lax.broadcasted_iota(I32, (K_TILE, Q_BLK), 0)
    tl = lax.broadcasted_iota(I32, (K_TILE, Q_BLK), 1)
    rel = sl - tl - o * Q_BLK
    nb = REL_BUCKETS // 2
    max_exact = nb // 2
    n = jnp.abs(rel)
    large = jnp.full(rel.shape, max_exact, I32)
    for bound in _BUCKET_BOUNDS:
        large = large + jnp.where(n >= bound, 1, 0)
    bucket = jnp.where(rel > 0, nb, 0) + jnp.where(n < max_exact, n, large)
    for h in range(SA_HEADS):
        acc = jnp.zeros(rel.shape, F32)
        for bkt in range(REL_BUCKETS):
            acc = jnp.where(bucket == bkt, rb_ref[bkt, h], acc)
        bt_out[0, h] = (acc - rb_ref[nb - 1, h]) * LOG2E


def _rel_bias_tiles(rel_bias):
    return pl.pallas_call(
        _rel_bias_kernel,
        grid=(N_BIAS_TILES,),
        in_specs=[pl.BlockSpec(memory_space=pltpu.SMEM)],
        out_specs=pl.BlockSpec((1, SA_HEADS, K_TILE, Q_BLK), lambda o: (o, 0, 0, 0)),
        out_shape=jax.ShapeDtypeStruct((N_BIAS_TILES, SA_HEADS, K_TILE, Q_BLK), F32),
        compiler_params=_cp(("parallel",)),
        name="rel_bias",
    )(rel_bias)


def _dsa_kernel(iq_ref, iwt_ref, ql_ref, ik_ref, c_ref, ct_ref, bt_ref, wuv_ref, o_out,
                key_scr, iq_scr, m_scr, alpha_scr, p_scr, acc_scr, *, topk):
    i = pl.program_id(0)
    t0 = i * Q_BLK
    n_tiles = i // 2 + 1
    H = SA_HEADS
    lane_q = lax.broadcasted_iota(I32, (Q_BLK, LANES), 1)
    first = lane_q < 64

    for p in range(IDX_HEADS // 2):
        pair = iq_ref[:, p * LANES:(p + 1) * LANES]
        swap = pltpu.roll(pair, 64, 1)
        rows = []
        for own, other in ((pair, swap), (swap, pair)):
            o_hi, o_lo = _split2(own)
            t_hi = other.astype(BF16)
            rows.append(jnp.concatenate([jnp.where(first, o_hi, t_hi), jnp.where(first, o_lo, jnp.zeros_like(o_lo))], axis=1))
        iq_scr[p] = jnp.concatenate(rows, axis=0)
    w_rows = [iwt_ref[h:h + 1, :] * IDX_SCALE for h in range(IDX_HEADS)]

    key_pos = lax.broadcasted_iota(I32, (K_TILE, Q_BLK), 0)
    q_lane = lax.broadcasted_iota(I32, (K_TILE, Q_BLK), 1)
    limit = t0 + (q_lane // 64 + 1) * 64

    def score_tile(j, carry):
        s0 = pl.multiple_of(j * K_TILE, K_TILE)
        keys = ik_ref[pl.ds(s0, K_TILE), :]
        sc = jnp.zeros((K_TILE, Q_BLK), F32)
        for p in range(IDX_HEADS // 2):
            d = _mm_nt(keys, iq_scr[p])
            sc = sc + jnp.maximum(d[:, :Q_BLK], 0.0) * w_rows[2 * p] + jnp.maximum(d[:, Q_BLK:], 0.0) * w_rows[2 * p + 1]
        sc = sc + 0.0
        bits = pltpu.bitcast(sc, I32)
        key = bits ^ ((bits >> 31) & 0x7FFFFFFF)
        key_scr[pl.ds(s0, K_TILE), :] = jnp.where(s0 + key_pos < limit, key, INT_MIN)
        return carry

    lax.fori_loop(0, n_tiles, score_tile, 0)

    n_steps = (n_tiles + R_TILES - 1) // R_TILES

    def pad_tile(j, carry):
        key_scr[pl.ds(pl.multiple_of(j * K_TILE, K_TILE), K_TILE), :] = jnp.full((K_TILE, Q_BLK), INT_MIN, I32)
        return carry

    lax.fori_loop(n_tiles, n_steps * R_TILES, pad_tile, 0)

    def count(pred):
        def step(js, acc):
            s0 = pl.multiple_of(js * (R_TILES * K_TILE), R_TILES * K_TILE)
            for sub in range(R_TILES * K_TILE // 64):
                hit = jnp.where(pred(key_scr[pl.ds(s0 + sub * 64, 64), :]), 1.0, 0.0)
                acc = acc + jnp.sum(hit.reshape(64 // SUBLANES, SUBLANES, Q_BLK), axis=0)
            return acc
        acc = lax.fori_loop(0, n_steps, step, jnp.zeros((SUBLANES, Q_BLK), F32))
        return jnp.sum(acc, axis=0, keepdims=True)

    def bit_step(it, cur):
        bit = lax.shift_left(jnp.int32(1), 31 - it)
        cand = (cur | bit) ^ INT_MIN
        cnt = count(lambda kt: kt >= cand)
        return jnp.where(cnt >= topk, cur | bit, cur)

    cur = lax.fori_loop(0, 32, bit_step, jnp.zeros((1, Q_BLK), I32))
    theta = cur ^ INT_MIN
    need = topk - count(lambda kt: kt > theta)

    lr = lax.broadcasted_iota(I32, (K_TILE, K_TILE), 0)
    lc = lax.broadcasted_iota(I32, (K_TILE, K_TILE), 1)
    lower = jnp.where(lr >= lc, 1.0, 0.0).astype(BF16)

    def mask_tile(j, carry):
        s0 = pl.multiple_of(j * K_TILE, K_TILE)
        kt = key_scr[pl.ds(s0, K_TILE), :]
        eq = kt == theta
        eq_f = jnp.where(eq, 1.0, 0.0)
        prefix = carry + _mm(lower, eq_f.astype(BF16))
        take = ((kt > theta) | (eq & (prefix <= need))) & (s0 + key_pos < limit)
        key_scr[pl.ds(s0, K_TILE), :] = pltpu.bitcast(jnp.where(take, 0.0, NEG_BIG), I32)
        return carry + jnp.sum(eq_f, axis=0, keepdims=True)

    lax.fori_loop(0, n_tiles, mask_tile, jnp.zeros((1, Q_BLK), F32))

    m_scr[...] = jnp.full(m_scr.shape, NEG_BIG, F32)
    acc_scr[...] = jnp.zeros(acc_scr.shape, F32)

    def attn_tile(j, carry, *, near):
        s0 = pl.multiple_of(j * K_TILE, K_TILE)
        c_t = c_ref[pl.ds(s0, K_TILE), :]
        addm = pltpu.bitcast(key_scr[pl.ds(s0, K_TILE), :], F32)
        for p in range(H // 2):
            lg2 = _mm_nt(c_t, ql_ref[2 * p:2 * p + 2].reshape(2 * Q_BLK, KV_RANK))
            for hh in range(2):
                h = 2 * p + hh
                ls = slice(h * Q_BLK, (h + 1) * Q_BLK)
                lg = lg2[:, hh * Q_BLK:(hh + 1) * Q_BLK] + addm
                if near:
                    lg = lg + bt_ref[i - 2 * j, h]
                m_prev = m_scr[:, ls]
                m_new = jnp.maximum(m_prev, jnp.max(lg, axis=0, keepdims=True))
                m_scr[:, ls] = m_new
                alpha_scr[:, ls] = jnp.exp2(m_prev - m_new)
                p_scr[:, ls] = jnp.exp2(lg - m_new).astype(BF16)
        pv = _mm(ct_ref[:, pl.ds(s0, K_TILE)], p_scr[...])
        acc_scr[...] = alpha_scr[...] * acc_scr[...] + pv
        return carry

    n_far = jnp.maximum(i - N_BIAS_TILES + 2, 0) // 2
    lax.fori_loop(0, n_far, functools.partial(attn_tile, near=False), 0)
    lax.fori_loop(n_far, n_tiles, functools.partial(attn_tile, near=True), 0)

    o_lat = (acc_scr[:KV_RANK, :] / acc_scr[KV_RANK:KV_RANK + 1, :]).T.astype(BF16)
    for p in range(H // 2):
        pair = jnp.concatenate([o_lat[2 * p * Q_BLK:(2 * p + 1) * Q_BLK], o_lat[(2 * p + 1) * Q_BLK:(2 * p + 2) * Q_BLK]],
                               axis=1)
        o_out[:, p * LANES:(p + 1) * LANES] = _mm(pair, wuv_ref[p])


ONES_ROWS = 16


def _dsa_operands(pp, c):
    ik = pp[:, 896:960]
    ik_hi = ik.astype(BF16)
    ik_lo = (ik - ik_hi.astype(F32)).astype(BF16)
    ik_packed = jnp.concatenate([ik_hi, ik_lo, ik_hi, jnp.zeros_like(ik_hi)], axis=1)
    iwt = jnp.transpose(pp[:, 960:968])
    ct_ext = jnp.concatenate([jnp.transpose(c), jnp.ones((ONES_ROWS, c.shape[0]), c.dtype)], axis=0)
    return iwt, ik_packed, c, ct_ext


def _dsa(pp, iwt, ik_packed, c, ct_ext, ql, bt, wuv2, *, topk):
    T = pp.shape[0]
    H = SA_HEADS
    assert T % (R_TILES * K_TILE) == 0 and N_BIAS_TILES * Q_BLK - (K_TILE - 1) >= _BUCKET_BOUNDS[-1]
    whole = pl.BlockSpec(memory_space=pltpu.VMEM)
    return pl.pallas_call(
        functools.partial(_dsa_kernel, topk=topk),
        grid=(T // Q_BLK,),
        in_specs=[pl.BlockSpec((Q_BLK, 512), lambda i: (i, 0)),
                  pl.BlockSpec((IDX_HEADS, Q_BLK), lambda i: (0, i)),
                  pl.BlockSpec((H, Q_BLK, KV_RANK), lambda i: (0, i, 0)),
                  whole, whole, whole, whole, whole],
        out_specs=pl.BlockSpec((Q_BLK, 1024), lambda i: (i, 0)),
        out_shape=jax.ShapeDtypeStruct((T, 1024), F32),
        scratch_shapes=[pltpu.VMEM((T, Q_BLK), I32),
                        pltpu.VMEM((IDX_HEADS // 2, 2 * Q_BLK, 2 * LANES), BF16),
                        pltpu.VMEM((1, H * Q_BLK), F32),
                        pltpu.VMEM((1, H * Q_BLK), F32),
                        pltpu.VMEM((K_TILE, H * Q_BLK), BF16),
                        pltpu.VMEM((KV_RANK + ONES_ROWS, H * Q_BLK), F32)],
        compiler_params=_cp(("parallel",), 56),
        name="dsa",
    )(pp, iwt, ql, ik_packed, c, ct_ext, bt, wuv2)


def _merge_kernel(y_ref, bonus_ref, g_ref, o_ref, ga_ref, gb_ref, x_ref, lnw_ref, lnb_ref, bd_ref,
                  wa_ref, wb_ref, wout_ref, h_out, m_scr):
    @pl.when(pl.program_id(1) == 0)
    def _():
        bd = bd_ref[...]

        def head_mean(x):
            hi, lo = _split2(x)
            return (_mm(hi, bd) + _mm(lo, bd)) * (1.0 / 64)

        y = y_ref[...]
        yc = y - head_mean(y)
        yn = yc * lax.rsqrt(head_mean(yc * yc) + RW_GN_EPS) * lnw_ref[...] + lnb_ref[...]
        ya = (yn + bonus_ref[...]) * g_ref[...]
        pa = _mm(ya.astype(BF16), wa_ref[...])
        pb = _mm(o_ref[...].astype(BF16), wb_ref[...])
        m = jax.nn.sigmoid(ga_ref[...]) * pa + jax.nn.sigmoid(gb_ref[...]) * pb
        m_scr[...] = m.astype(BF16)

    h_out[...] = x_ref[...] + _mm(m_scr[...], wout_ref[...])


def _merge(y, bonus, g, o, pm, x, ln_w, ln_b, bd, wa, wb, wout, *, tm, tn):
    T, D = x.shape
    W = RW_WIDTH
    row = lambda a: a.reshape(1, -1)
    act = pl.BlockSpec((tm, W), lambda i, j: (i, 0))
    full = lambda shape: pl.BlockSpec(shape, lambda i, j: (0,) * len(shape))
    return pl.pallas_call(
        _merge_kernel,
        grid=(T // tm, D // tn),
        in_specs=[act, act, act, act,
                  pl.BlockSpec((tm, D), lambda i, j: (i, 0)),
                  pl.BlockSpec((tm, D), lambda i, j: (i, 1)),
                  pl.BlockSpec((tm, tn), lambda i, j: (i, j)),
                  full((1, W)), full((1, W)), full((W, W)), full((W, D)), full((W, D)),
                  pl.BlockSpec((D, tn), lambda i, j: (0, j))],
        out_specs=pl.BlockSpec((tm, tn), lambda i, j: (i, j)),
        out_shape=jax.ShapeDtypeStruct((T, D), F32),
        scratch_shapes=[pltpu.VMEM((tm, D), BF16)],
        compiler_params=_cp(("parallel", "arbitrary"), 56),
        name="merge",
    )(y, bonus, g, o, pm, pm, x, row(ln_w), row(ln_b), bd, wa, wb, wout)


def _sort16_pairs():
    n, pairs, p = 16, [], 1
    while p < n:
        k = p
        while k >= 1:
            for j in range(k % p, n - k, 2 * k):
                for i in range(min(k, n - j - k)):
                    if (i + j) // (2 * p) == (i + j + k) // (2 * p):
                        pairs.append((i + j, i + j + k))
            k //= 2
        p *= 2
    return pairs


_SORT16 = _sort16_pairs()
PEER_MARGIN = 1.0 - 2.0 ** -20


def _exchange(a, i, j):
    a[i], a[j] = jnp.maximum(a[i], a[j]), jnp.minimum(a[i], a[j])


def _bitonic_sort16(a):
    for dist in (8, 4, 2, 1):
        for i in range(PEER_TOPK):
            if not i & dist:
                _exchange(a, i, i + dist)
    return a


def _top16_merge(x, y):
    return _bitonic_sort16([jnp.maximum(x[i], y[PEER_TOPK - 1 - i]) for i in range(PEER_TOPK)])


def _peer_select_kernel(qp_ref, keys_ref, et_out, tau_out):
    tm = qp_ref.shape[1]
    sub = lax.broadcasted_iota(I32, (SUBLANES, tm), 0)
    zero = jnp.zeros((SUBLANES, tm), F32)

    def per_head(h, packed):
        packed = list(packed)
        for c in range(2):
            hc = 2 * h + c
            s_t = _dot3(keys_ref[hc], qp_ref[hc], _mm_nt)
            e = jnp.exp(s_t - jnp.max(s_t, axis=0, keepdims=True))
            et_out[hc] = e
            a = [e[g * SUBLANES:(g + 1) * SUBLANES, :] for g in range(PEER_KEYS // SUBLANES)]
            for i, j in _SORT16:
                _exchange(a, i, j)
            for shift in (4, 2, 1):
                a = _top16_merge(a, [pltpu.roll(t, shift, 0) for t in a])
            for i in range(PEER_TOPK):
                packed[c * PEER_TOPK + i] = jnp.where(sub == h, a[i], packed[c * PEER_TOPK + i])
        return tuple(packed)

    packed = lax.fori_loop(0, PEER_HEADS, per_head, (zero,) * (2 * PEER_TOPK))
    top_a, top_b = packed[:PEER_TOPK], packed[PEER_TOPK:]

    def run(i, n):
        return [top_a[i] * top_b[j] for j in range(n)]

    g1 = run(0, 16)
    g2 = _bitonic_sort16(run(1, 8) + [top_a[i] * top_b[0] for i in range(PEER_TOPK - 1, 7, -1)])
    g3 = run(2, 5) + run(3, 4) + run(4, 3) + run(5, 2) + run(6, 2)
    for i, j in _SORT16:
        _exchange(g3, i, j)
    top = _top16_merge(_top16_merge(_top16_merge(g1, g2), g3), run(7, 2) + [zero] * 14)
    z = top[0]
    for t in top[1:]:
        z = z + t
    zinv = 1.0 / z
    tau_out[...] = top[PEER_TOPK - 1] * PEER_MARGIN * zinv
    for h in range(PEER_HEADS):
        et_out[2 * h] = et_out[2 * h] * zinv[h:h + 1, :]


def _peer_select(qp, keys, *, tm):
    T = qp.shape[1]
    return pl.pallas_call(
        _peer_select_kernel,
        grid=(T // tm,),
        in_specs=[pl.BlockSpec((2 * PEER_HEADS, tm, LANES), lambda i: (0, i, 0)),
                  pl.BlockSpec((2 * PEER_HEADS, PEER_KEYS, LANES), lambda i: (0, 0, 0))],
        out_specs=[pl.BlockSpec((2 * PEER_HEADS, PEER_KEYS, tm), lambda i: (0, 0, i)),
                   pl.BlockSpec((SUBLANES, tm), lambda i: (0, i))],
        out_shape=[jax.ShapeDtypeStruct((2 * PEER_HEADS, PEER_KEYS, T), F32),
                   jax.ShapeDtypeStruct((PEER_HEADS, T), F32)],
        compiler_params=_cp(("parallel",)),
        name="peer_select",
    )(qp, keys)


def _peer_kernel(h_ref, gf_ref, gout_ref, u_ref, vt_ref, et_ref, tau_ref, out_ref, hn_scr, acc_scr, *, e_tile):
    j = pl.program_id(1)
    tm = h_ref.shape[0]

    @pl.when(j == 0)
    def _():
        h = h_ref[...]
        hn = h * lax.rsqrt(jnp.mean(h * h, axis=-1, keepdims=True) + EPS) * gf_ref[...]
        hn_scr[...] = hn.astype(BF16)
        acc_scr[...] = jnp.zeros_like(acc_scr)

    pre = _mm_nt(u_ref[...], hn_scr[...])
    act = 0.5 * pre * (1.0 + lax.erf(pre * (2.0 ** -0.5)))
    parts = []
    for ai in range(e_tile // PEER_KEYS):
        a = j * (e_tile // PEER_KEYS) + ai
        gate = jnp.zeros((PEER_KEYS, tm), F32)
        for h in range(PEER_HEADS):
            w = et_ref[2 * h + 1] * et_ref[2 * h, pl.ds(a, 1), :]
            gate = gate + jnp.where(w >= tau_ref[h:h + 1, :], w, 0.0)
        parts.append((gate * act[ai * PEER_KEYS:(ai + 1) * PEER_KEYS, :]).astype(BF16))
    acc_scr[...] += _mm(vt_ref[...], jnp.concatenate(parts, axis=0))

    @pl.when(j == pl.num_programs(1) - 1)
    def _():
        h3 = h_ref[...] + acc_scr[...].T
        out_ref[...] = h3 * lax.rsqrt(jnp.mean(h3 * h3, axis=-1, keepdims=True) + EPS) * gout_ref[...]


def _peer(h1, g_ffn, g_out, u, vt, et, tau, *, tm, e_tile):
    T, D = h1.shape
    E = u.shape[0]
    return pl.pallas_call(
        functools.partial(_peer_kernel, e_tile=e_tile),
        grid=(T // tm, E // e_tile),
        in_specs=[pl.BlockSpec((tm, D), lambda i, j: (i, 0)),
                  pl.BlockSpec((1, D), lambda i, j: (0, 0)),
                  pl.BlockSpec((1, D), lambda i, j: (0, 0)),
                  pl.BlockSpec((e_tile, D), lambda i, j: (j, 0)),
                  pl.BlockSpec((D, e_tile), lambda i, j: (0, j)),
                  pl.BlockSpec((2 * PEER_HEADS, PEER_KEYS, tm), lambda i, j: (0, 0, i)),
                  pl.BlockSpec((PEER_HEADS, tm), lambda i, j: (0, i))],
        out_specs=pl.BlockSpec((tm, D), lambda i, j: (i, 0)),
        out_shape=jax.ShapeDtypeStruct((T, D), F32),
        scratch_shapes=[pltpu.VMEM((tm, D), BF16), pltpu.VMEM((D, tm), F32)],
        compiler_params=_cp(("parallel", "arbitrary"), 56),
        name="peer",
    )(h1, g_ffn.reshape(1, D), g_out.reshape(1, D), u, vt, et, tau)


def _pad_cols(a, n):
    return jnp.pad(a, ((0, 0), (0, n - a.shape[1])))


def _block_diag2(a, b):
    za = jnp.zeros((a.shape[0], b.shape[1]), a.dtype)
    zb = jnp.zeros((b.shape[0], a.shape[1]), a.dtype)
    return jnp.concatenate([jnp.concatenate([a, za], axis=1), jnp.concatenate([zb, b], axis=1)], axis=0)


def kernel(x, norm_mix_g, w_in, shift_mix, w0, w_decay_up, a0, w_icl_up, w_gate_up, k_k, k_a, r_k, ln_x_w, ln_x_b,
           kv_norm_g, w_uk, w_uv, rel_bias, w_branch_rwkv, w_branch_dsa, w_out, norm_ffn_g, w_peer_query,
           peer_sub_keys, peer_u, peer_v, norm_final_g):
    B, T, D = x.shape
    assert B == 1 and D == D_MODEL and T % (2 * K_TILE) == 0 and norm_mix_g.shape[0] == 1
    topk = min(TOPK_MAX, T // 4)
    x2 = x[0]
    w = w_in[0]
    sm = shift_mix[0]
    o_zw, o_za, o_zg, o_q, o_ckv, o_iq, o_ik, o_iw, o_g = 3072, 3136, 3200, 3360, 4384, 4640, 5152, 5216, 5224
    w_main = jnp.concatenate([w[:, o_g:o_g + 4096], w[:, :3072], w[:, o_q:o_q + 1024], w[:, o_ckv:o_ckv + 256]],
                             axis=1).astype(BF16)
    w_prec = jnp.concatenate([w[:, o_iq:o_iq + 512], _pad_cols(w[:, o_zg:o_zg + 160], 256), w[:, o_zw:o_zw + 128],
                              _pad_cols(w[:, o_ik:o_ik + 72], 128)], axis=1)
    tm = min(512, T)
    pm = _proj(x2, norm_mix_g[0], w_main, precise=False, tm=tm, tn=768, name="proj_main")
    pp = _proj(x2, norm_mix_g[0], w_prec, precise=True, tm=tm, tn=512, name="proj_precise")

    head_id = jnp.arange(RW_WIDTH) // 64
    bd = (head_id[:, None] == head_id[None, :]).astype(BF16)
    mixes = [sm[:1024], sm[1024:2048], sm[2048:3072], sm[o_zw:o_zw + 128],
             jnp.pad(sm[o_zg:o_zg + 160], (0, 96))]
    wd = jnp.pad(w_decay_up[0], ((0, 64), (0, 0)))
    wi = jnp.pad(w_icl_up[0], ((64, 0), (0, 0)))
    wg = jnp.pad(w_gate_up[0], ((0, 96), (0, 0)))
    r, k4, v, kkn, b, lw, g, bonus = _rw_prep(pm, pp, mixes, w0[0], wd, a0[0], wi, wg, k_k[0], k_a[0],
                                              r_k[0].reshape(-1), bd, tm=min(256, T))
    nch = T // RW_CHUNK
    p_mat, q_mat, rp, y0 = _rw_chunk(r, k4, v, kkn, b, lw, G=min(8, nch))
    y = _rw_scan(p_mat, q_mat, rp, y0)

    wuk_t = jnp.transpose(w_uk[0], (0, 2, 1))
    wuk2 = jnp.stack([_block_diag2(wuk_t[2 * p], wuk_t[2 * p + 1]) for p in range(SA_HEADS // 2)]).astype(BF16)
    wuv2 = jnp.stack([_block_diag2(w_uv[0][2 * p], w_uv[0][2 * p + 1]) for p in range(SA_HEADS // 2)]).astype(BF16)
    c, ql = _dsa_prep(pm, kv_norm_g[0], wuk2, tm=min(512, T))
    o_dsa = _dsa(pp, *_dsa_operands(pp, c), ql, _rel_bias_tiles(rel_bias), wuv2, topk=topk)

    h1 = _merge(y, bonus, g, o_dsa, pm, x2, ln_x_w[0], ln_x_b[0], bd,
                w_branch_rwkv[0].astype(BF16), w_branch_dsa[0].astype(BF16), w_out[0].astype(BF16),
                tm=min(256, T), tn=512)

    qp = _proj(h1, norm_ffn_g[0], w_peer_query[0], precise=True, tm=tm, tn=512, name="peer_query", lane_groups=True)
    keys = peer_sub_keys[0].reshape(2 * PEER_HEADS, PEER_KEYS, LANES)
    et, tau = _peer_select(qp, keys, tm=128)
    u = peer_u[0].astype(BF16)
    vt = jnp.transpose(peer_v[0]).astype(BF16)
    out = _peer(h1, norm_ffn_g[0], norm_final_g, u, vt, et, tau, tm=min(512, T), e_tile=512)
    return out[None]
```

```python
import functools
import math

import numpy as np
import jax
import jax.numpy as jnp
from jax import lax
from jax.experimental import pallas as pl
from jax.experimental.pallas import tpu as pltpu

F32, BF16, I32 = jnp.float32, jnp.bfloat16, jnp.int32

D_MODEL = 2048
EPS = 1e-6
RW_WIDTH = 1024
RW_GN_EPS = 64e-5
RW_CHUNK = 128
RW_LOCKSTEP = 4
SA_HEADS = 16
SA_SCALE = 64 ** -0.5
KV_RANK = 256
IDX_HEADS = 8
IDX_SCALE = (8 * 64) ** -0.5
TOPK_MAX = 256
REL_BUCKETS = 32
REL_MAX_DIST = 1024
PEER_HEADS = 8
PEER_KEYS = 128
PEER_TOPK = 16
LANES = 128
SUBLANES = 8
NEG_BIG = -1e30
INT_MIN = -2 ** 31

Q_BLK = 128
K_TILE = 256
N_BIAS_TILES = 7
R_TILES = 4
LOG2E = math.log2(math.e)


def _cp(sem, vmem_mb=48):
    return pltpu.CompilerParams(dimension_semantics=sem, vmem_limit_bytes=vmem_mb * 1024 * 1024)


def _bf(x):
    return x.astype(BF16)


def _split2(x):
    hi = x.astype(BF16)
    lo = (x - hi.astype(F32)).astype(BF16)
    return hi, lo


def _mm(a, b):
    return jnp.dot(a, b, preferred_element_type=F32)


def _mm_nt(a, b):
    return lax.dot_general(a, b, (((1,), (1,)), ((), ())), preferred_element_type=F32)


def _mm_tn(a, b):
    return lax.dot_general(a, b, (((0,), (0,)), ((), ())), preferred_element_type=F32)


def _d3(a_s, b_s, mm=_mm):
    return mm(a_s[0], b_s[0]) + (mm(a_s[0], b_s[1]) + mm(a_s[1], b_s[0]))


def _dot3(a, b, mm=_mm):
    return _d3(_split2(a), _split2(b), mm)


def _proj_kernel(x_ref, g_ref, w_ref, o_ref, xh_ref, xl_ref, *, precise, lane_groups):
    @pl.when(pl.program_id(1) == 0)
    def _():
        x = x_ref[...]
        xn = x * lax.rsqrt(jnp.mean(x * x, axis=-1, keepdims=True) + EPS) * g_ref[...]
        hi = xn.astype(BF16)
        xh_ref[...] = hi
        if precise:
            xl_ref[...] = (xn - hi.astype(F32)).astype(BF16)

    if precise:
        wh, wl = _split2(w_ref[...])
        xh = xh_ref[...]
        res = _mm(xh, wh) + (_mm(xh, wl) + _mm(xl_ref[...], wh))
    else:
        res = _mm(xh_ref[...], w_ref[...])
    if lane_groups:
        for k in range(res.shape[1] // LANES):
            o_ref[k] = res[:, k * LANES:(k + 1) * LANES]
    else:
        o_ref[...] = res


def _proj(x, g, w, *, precise, tm, tn, name, lane_groups=False):
    T, D = x.shape
    N = w.shape[1]
    lo_rows = tm if precise else SUBLANES * 2
    if lane_groups:
        out_spec = pl.BlockSpec((tn // LANES, tm, LANES), lambda i, j: (j, i, 0))
        out_shape = jax.ShapeDtypeStruct((N // LANES, T, LANES), F32)
    else:
        out_spec = pl.BlockSpec((tm, tn), lambda i, j: (i, j))
        out_shape = jax.ShapeDtypeStruct((T, N), F32)
    return pl.pallas_call(
        functools.partial(_proj_kernel, precise=precise, lane_groups=lane_groups),
        grid=(T // tm, N // tn),
        in_specs=[pl.BlockSpec((tm, D), lambda i, j: (i, 0)),
                  pl.BlockSpec((1, D), lambda i, j: (0, 0)),
                  pl.BlockSpec((D, tn), lambda i, j: (0, j))],
        out_specs=out_spec,
        out_shape=out_shape,
        scratch_shapes=[pltpu.VMEM((tm, D), BF16), pltpu.VMEM((lo_rows, D), BF16)],
        compiler_params=_cp(("parallel", "arbitrary")),
        name=name,
    )(x, g.reshape(1, D), w)


def _rw_prep_kernel(r_ref, k_ref, v_ref, wa_ref, zg_ref, rp_ref, kp_ref, vp_ref, wap_ref, zgp_ref,
                    mr_ref, mk_ref, mv_ref, mwa_ref, mg_ref, w0_ref, wd_ref, a0_ref, wi_ref, wg_ref,
                    kk_ref, ka_ref, rk_ref, bd_ref,
                    r_out, k_out, v_out, kkn_out, b_out, lw_out, g_out, bonus_out):
    first = pl.program_id(0) == 0

    def shift(p_ref, prev_ref, mix_ref):
        p = p_ref[...]
        last = jnp.where(first, 0.0, prev_ref[SUBLANES - 1:SUBLANES, :])
        row = lax.broadcasted_iota(I32, p.shape, 0)
        prev = jnp.where(row == 0, last, pltpu.roll(p, 1, 0))
        return p + (prev - p) * mix_ref[...]

    r = shift(r_ref, rp_ref, mr_ref)
    k = shift(k_ref, kp_ref, mk_ref)
    v = shift(v_ref, vp_ref, mv_ref)
    zwa = shift(wa_ref, wap_ref, mwa_ref)
    zg = shift(zg_ref, zgp_ref, mg_ref)

    nd = -(w0_ref[...] + _dot3(jnp.tanh(zwa), wd_ref[...]))
    softplus = jnp.maximum(nd, 0.0) + jnp.log1p(jnp.exp(-jnp.abs(nd)))
    lw = -jnp.exp(-softplus - 0.5)
    a = jax.nn.sigmoid(a0_ref[...] + _dot3(zwa, wi_ref[...]))
    g = _dot3(jax.nn.sigmoid(zg), wg_ref[...])

    bd = bd_ref[...]

    def head_sum(x):
        hi, lo = _split2(x)
        return _mm(hi, bd) + _mm(lo, bd)

    kk = k * kk_ref[...]
    kkn = kk / jnp.maximum(jnp.sqrt(head_sum(kk * kk)), 1e-12)
    k4 = k * (1.0 + (a - 1.0) * ka_ref[...])
    r_out[...] = r
    k_out[...] = k4
    v_out[...] = v
    kkn_out[...] = kkn
    b_out[...] = kkn * a
    lw_out[...] = lw
    g_out[...] = g
    bonus_out[...] = head_sum(r * k4 * rk_ref[...]) * v


def _rw_prep(pm, pp, mixes, w0, wd, a0, wi, wg, k_k, k_a, r_k, bd, *, tm):
    T = pm.shape[0]
    W = RW_WIDTH
    pb = tm // SUBLANES

    def cur(width, col):
        return pl.BlockSpec((tm, width), lambda i: (i, col))

    def prev(width, col):
        return pl.BlockSpec((SUBLANES, width), lambda i: (jnp.maximum(i * pb - 1, 0), col))

    def full(shape):
        return pl.BlockSpec(shape, lambda i: (0,) * len(shape))

    row = lambda a: a.reshape(1, -1)
    out_sd = jax.ShapeDtypeStruct((T, W), F32)
    return pl.pallas_call(
        _rw_prep_kernel,
        grid=(T // tm,),
        in_specs=[cur(W, 4), cur(W, 5), cur(W, 6), cur(128, 6), cur(256, 2),
                  prev(W, 4), prev(W, 5), prev(W, 6), prev(128, 6), prev(256, 2),
                  full((1, W)), full((1, W)), full((1, W)), full((1, 128)), full((1, 256)),
                  full((1, W)), full((128, W)), full((1, W)), full((128, W)), full((256, W)),
                  full((1, W)), full((1, W)), full((1, W)), full((W, W))],
        out_specs=[pl.BlockSpec((tm, W), lambda i: (i, 0))] * 8,
        out_shape=[out_sd] * 8,
        compiler_params=_cp(("parallel",)),
        name="rw_prep",
    )(pm, pm, pm, pp, pp, pm, pm, pm, pp, pp,
      *[row(m) for m in mixes], row(w0), wd, row(a0), wi, wg, row(k_k), row(k_a), row(r_k), bd)


def _rw_chunk_kernel(r_ref, k_ref, v_ref, kkn_ref, b_ref, lw_ref, p_out, q_out, rp_out, y0_out, *, G):
    C = RW_CHUNK
    row = lax.broadcasted_iota(I32, (C, C), 0)
    col = lax.broadcasted_iota(I32, (C, C), 1)
    incl = row >= col
    strict = row > col
    eye = row == col
    same_head = (row // 64) == (col // 64)
    tri = jnp.where(incl, 1.0, 0.0).astype(BF16)
    lane = lax.broadcasted_iota(I32, (C, LANES), 1)
    eye_f = jnp.where(eye, 1.0, 0.0)

    head_lanes = (lane < 64, lane >= 64)
    U = RW_LOCKSTEP

    def body(cu, carry):
        ch = [cu * U + u for u in range(U)]
        sls = [pl.ds(pl.multiple_of(c * C, C), C) for c in ch]
        pre = []
        for sl in sls:
            r, k4, v, kkn, b, lw = (ref[sl, :] for ref in (r_ref, k_ref, v_ref, kkn_ref, b_ref, lw_ref))
            h1 = lw.astype(BF16)
            r1 = lw - h1.astype(F32)
            h2 = r1.astype(BF16)
            h3 = (r1 - h2.astype(F32)).astype(BF16)
            L = _mm(tri, h1) + (_mm(tri, h2) + _mm(tri, h3))
            Ltot = L[C - 1:C, :]
            Lr = L - L[C // 2 - 1:C // 2, :]
            eneg = jnp.exp(-Lr)
            tail = jnp.exp(Ltot - L)
            pre.append(dict(
                rhat=r * jnp.exp(Lr), ahat=-kkn * jnp.exp(Lr - lw), kt=_bf(k4 * eneg), bt=_bf(b * eneg),
                rabs=r * jnp.exp(L), aabs=-kkn * jnp.exp(L - lw), kbar=_bf(k4 * tail), bbar=_bf(b * tail),
                gam=jnp.exp(Ltot), vb=_bf(v)))
        inst = [(u, hh) for u in range(U) for hh in range(2)]
        am = [_bf(jnp.where(head_lanes[hh], pre[u]["ahat"], 0.0)) for u, hh in inst]
        rm = [_bf(jnp.where(head_lanes[hh], pre[u]["rhat"], 0.0)) for u, hh in inst]
        a_ab = [jnp.where(strict, _mm_nt(am[k], pre[u]["bt"]), 0.0) for k, (u, hh) in enumerate(inst)]
        a_ak = [_bf(jnp.where(strict, _mm_nt(am[k], pre[u]["kt"]), 0.0)) for k, (u, hh) in enumerate(inst)]
        m_rb = [_bf(jnp.where(incl, _mm_nt(rm[k], pre[u]["bt"]), 0.0)) for k, (u, hh) in enumerate(inst)]
        m_rk = [_bf(jnp.where(incl, _mm_nt(rm[k], pre[u]["kt"]), 0.0)) for k, (u, hh) in enumerate(inst)]
        n = a_ab
        x = [eye_f + t for t in n]
        for _ in range(6):
            nb = [_bf(t) for t in n]
            n = [_mm(t, t) for t in nb]
            x = [xi + _mm(_bf(xi), _bf(ni)) for xi, ni in zip(x, n)]
        xb = [_bf(t) for t in x]
        akv = [_bf(_mm(a_ak[k], pre[u]["vb"])) for k, (u, hh) in enumerate(inst)]
        ah_h = [_mm(xb[k], _bf(jnp.where(head_lanes[hh], pre[u]["aabs"], 0.0))) for k, (u, hh) in enumerate(inst)]
        u0_h = [_mm(xb[k], akv[k]) for k in range(len(inst))]
        y0_h = [_mm(m_rk[k], pre[u]["vb"]) + _mm(m_rb[k], _bf(u0_h[k])) for k, (u, hh) in enumerate(inst)]
        rp_h = [jnp.where(head_lanes[hh], pre[u]["rabs"], 0.0) + _mm(m_rb[k], _bf(ah_h[k]))
                for k, (u, hh) in enumerate(inst)]
        for u in range(U):
            lo, hi = 2 * u, 2 * u + 1
            ah = _bf(ah_h[lo] + ah_h[hi])
            u0 = _bf(jnp.where(head_lanes[0], u0_h[lo], u0_h[hi]))
            pu = pre[u]
            p_out[0, ch[u]] = jnp.where(eye, pu["gam"], 0.0) + jnp.where(same_head, _mm_tn(pu["bbar"], ah), 0.0)
            q_out[0, ch[u]] = jnp.where(same_head, _mm_tn(pu["kbar"], pu["vb"]) + _mm_tn(pu["bbar"], u0), 0.0)
            rp_out[sls[u], :] = rp_h[lo] + rp_h[hi]
            y0_out[sls[u], :] = jnp.where(head_lanes[0], y0_h[lo], y0_h[hi])
        return carry

    lax.fori_loop(0, G // U, body, 0)


def _rw_chunk(r, k4, v, kkn, b, lw, *, G):
    T = r.shape[0]
    C = RW_CHUNK
    nch = T // C
    rows = G * C
    cur = pl.BlockSpec((rows, LANES), lambda p, g: (g, p))
    mat = pl.BlockSpec((1, G, C, C), lambda p, g: (p, g, 0, 0))
    return pl.pallas_call(
        functools.partial(_rw_chunk_kernel, G=G),
        grid=(RW_WIDTH // LANES, nch // G),
        in_specs=[cur] * 6,
        out_specs=[mat, mat, cur, cur],
        out_shape=[jax.ShapeDtypeStruct((RW_WIDTH // LANES, nch, C, C), F32)] * 2
        + [jax.ShapeDtypeStruct((T, RW_WIDTH), F32)] * 2,
        compiler_params=_cp(("parallel", "parallel")),
        name="rw_chunk",
    )(r, k4, v, kkn, b, lw)


def _rw_scan_kernel(p_ref, q_ref, rp_ref, y0_ref, y_out, s_scr):
    @pl.when(pl.program_id(0) == 0)
    def _():
        s_scr[...] = jnp.zeros_like(s_scr)

    for p in range(RW_WIDTH // LANES):
        ls = slice(p * LANES, (p + 1) * LANES)
        s0 = _bf(s_scr[p])
        y_out[:, ls] = _mm(_bf(rp_ref[:, ls]), s0) + y0_ref[:, ls]
        s_scr[p] = _mm(_bf(p_ref[p, 0]), s0) + q_ref[p, 0]


def _rw_scan(p_mat, q_mat, rp, y0):
    T = rp.shape[0]
    C = RW_CHUNK
    npair = RW_WIDTH // LANES
    mat = pl.BlockSpec((npair, 1, C, C), lambda c: (0, c, 0, 0))
    cur = pl.BlockSpec((C, RW_WIDTH), lambda c: (c, 0))
    return pl.pallas_call(
        _rw_scan_kernel,
        grid=(T // C,),
        in_specs=[mat, mat, cur, cur],
        out_specs=cur,
        out_shape=jax.ShapeDtypeStruct((T, RW_WIDTH), F32),
        scratch_shapes=[pltpu.VMEM((npair, C, C), F32)],
        compiler_params=_cp(("arbitrary",)),
        name="rw_scan",
    )(p_mat, q_mat, rp, y0)


def _dsa_prep_kernel(q_ref, ckv_ref, g_ref, wuk_ref, c_out, ql_out):
    ckv = ckv_ref[...]
    c = ckv * lax.rsqrt(jnp.mean(ckv * ckv, axis=-1, keepdims=True) + EPS) * g_ref[...]
    c_out[...] = c.astype(BF16)
    for p in range(SA_HEADS // 2):
        qp = q_ref[:, p * LANES:(p + 1) * LANES].astype(BF16)
        ql = _mm(qp, wuk_ref[p]) * (SA_SCALE * LOG2E)
        ql_out[2 * p] = ql[:, :KV_RANK].astype(BF16)
        ql_out[2 * p + 1] = ql[:, KV_RANK:].astype(BF16)


def _dsa_prep(pm, kv_norm_g, wuk2, *, tm):
    T = pm.shape[0]
    return pl.pallas_call(
        _dsa_prep_kernel,
        grid=(T // tm,),
        in_specs=[pl.BlockSpec((tm, 1024), lambda i: (i, 7)),
                  pl.BlockSpec((tm, KV_RANK), lambda i: (i, 32)),
                  pl.BlockSpec((1, KV_RANK), lambda i: (0, 0)),
                  pl.BlockSpec((SA_HEADS // 2, LANES, 2 * KV_RANK), lambda i: (0, 0, 0))],
        out_specs=[pl.BlockSpec((tm, KV_RANK), lambda i: (i, 0)),
                   pl.BlockSpec((SA_HEADS, tm, KV_RANK), lambda i: (0, i, 0))],
        out_shape=[jax.ShapeDtypeStruct((T, KV_RANK), BF16),
                   jax.ShapeDtypeStruct((SA_HEADS, T, KV_RANK), BF16)],
        compiler_params=_cp(("parallel",)),
        name="dsa_prep",
    )(pm, pm, kv_norm_g.reshape(1, KV_RANK), wuk2)


def _bucket_bounds():
    nb = REL_BUCKETS // 2
    max_exact = nb // 2
    n = np.arange(1, 2 * REL_MAX_DIST, dtype=np.int64)
    large = max_exact + (np.log(n.astype(np.float32) / max_exact)
                         / math.log(REL_MAX_DIST / max_exact) * (nb - max_exact)).astype(np.int32)
    large = np.minimum(large, nb - 1)
    bounds = [int(n[np.argmax(large >= b)]) for b in range(max_exact + 1, nb)]
    assert all(b2 > b1 for b1, b2 in zip(bounds, bounds[1:]))
    return bounds


_BUCKET_BOUNDS = _bucket_bounds()


def _rel_bias_kernel(rb_ref, bt_out):
    o = pl.program_id(0)
    sl = lax.broadcasted_iota(I32, (K_TILE, Q_BLK), 0)
    tl = lax.broadcasted_iota(I32, (K_TILE, Q_BLK), 1)
    rel = sl - tl - o * Q_BLK
    nb = REL_BUCKETS // 2
    max_exact = nb // 2
    n = jnp.abs(rel)
    large = jnp.full(rel.shape, max_exact, I32)
    for bound in _BUCKET_BOUNDS:
        large = large + jnp.where(n >= bound, 1, 0)
    bucket = jnp.where(rel > 0, nb, 0) + jnp.where(n < max_exact, n, large)
    for h in range(SA_HEADS):
        acc = jnp.zeros(rel.shape, F32)
        for bkt in range(REL_BUCKETS):
            acc = jnp.where(bucket == bkt, rb_ref[bkt, h], acc)
        bt_out[0, h] = (acc - rb_ref[nb - 1, h]) * LOG2E


def _rel_bias_tiles(rel_bias):
    return pl.pallas_call(
        _rel_bias_kernel,
        grid=(N_BIAS_TILES,),
        in_specs=[pl.BlockSpec(memory_space=pltpu.SMEM)],
        out_specs=pl.BlockSpec((1, SA_HEADS, K_TILE, Q_BLK), lambda o: (o, 0, 0, 0)),
        out_shape=jax.ShapeDtypeStruct((N_BIAS_TILES, SA_HEADS, K_TILE, Q_BLK), F32),
        compiler_params=_cp(("parallel",)),
        name="rel_bias",
    )(rel_bias)


def _dsa_kernel(iq_ref, iwt_ref, ql_ref, ik_ref, c_ref, ct_ref, bt_ref, wuv_ref, o_out,
                key_scr, iq_scr, m_scr, alpha_scr, p_scr, acc_scr, *, topk):
    i = pl.program_id(0)
    t0 = i * Q_BLK
    n_tiles = i // 2 + 1
    H = SA_HEADS
    lane_q = lax.broadcasted_iota(I32, (Q_BLK, LANES), 1)
    first = lane_q < 64

    for p in range(IDX_HEADS // 2):
        pair = iq_ref[:, p * LANES:(p + 1) * LANES]
        swap = pltpu.roll(pair, 64, 1)
        rows = []
        for own, other in ((pair, swap), (swap, pair)):
            o_hi, o_lo = _split2(own)
            t_hi = other.astype(BF16)
            rows.append(jnp.concatenate([jnp.where(first, o_hi, t_hi), jnp.where(first, o_lo, jnp.zeros_like(o_lo))], axis=1))
        iq_scr[p] = jnp.concatenate(rows, axis=0)
    w_rows = [iwt_ref[h:h + 1, :] * IDX_SCALE for h in range(IDX_HEADS)]

    key_pos = lax.broadcasted_iota(I32, (K_TILE, Q_BLK), 0)
    q_lane = lax.broadcasted_iota(I32, (K_TILE, Q_BLK), 1)
    limit = t0 + (q_lane // 64 + 1) * 64

    def score_tile(j, carry):
        s0 = pl.multiple_of(j * K_TILE, K_TILE)
        keys = ik_ref[pl.ds(s0, K_TILE), :]
        sc = jnp.zeros((K_TILE, Q_BLK), F32)
        for p in range(IDX_HEADS // 2):
            d = _mm_nt(keys, iq_scr[p])
            sc = sc + jnp.maximum(d[:, :Q_BLK], 0.0) * w_rows[2 * p] + jnp.maximum(d[:, Q_BLK:], 0.0) * w_rows[2 * p + 1]
        sc = sc + 0.0
        bits = pltpu.bitcast(sc, I32)
        key = bits ^ ((bits >> 31) & 0x7FFFFFFF)
        key_scr[pl.ds(s0, K_TILE), :] = jnp.where(s0 + key_pos < limit, key, INT_MIN)
        return carry

    lax.fori_loop(0, n_tiles, score_tile, 0)

    n_steps = (n_tiles + R_TILES - 1) // R_TILES

    def pad_tile(j, carry):
        key_scr[pl.ds(pl.multiple_of(j * K_TILE, K_TILE), K_TILE), :] = jnp.full((K_TILE, Q_BLK), INT_MIN, I32)
        return carry

    lax.fori_loop(n_tiles, n_steps * R_TILES, pad_tile, 0)

    def count(pred):
        def step(js, acc):
            s0 = pl.multiple_of(js * (R_TILES * K_TILE), R_TILES * K_TILE)
            for sub in range(R_TILES * K_TILE // 64):
                hit = jnp.where(pred(key_scr[pl.ds(s0 + sub * 64, 64), :]), 1.0, 0.0)
                acc = acc + jnp.sum(hit.reshape(64 // SUBLANES, SUBLANES, Q_BLK), axis=0)
            return acc
        acc = lax.fori_loop(0, n_steps, step, jnp.zeros((SUBLANES, Q_BLK), F32))
        return jnp.sum(acc, axis=0, keepdims=True)

    def bit_step(it, cur):
        bit = lax.shift_left(jnp.int32(1), 31 - it)
        cand = (cur | bit) ^ INT_MIN
        cnt = count(lambda kt: kt >= cand)
        return jnp.where(cnt >= topk, cur | bit, cur)

    cur = lax.fori_loop(0, 32, bit_step, jnp.zeros((1, Q_BLK), I32))
    theta = cur ^ INT_MIN
    need = topk - count(lambda kt: kt > theta)

    lr = lax.broadcasted_iota(I32, (K_TILE, K_TILE), 0)
    lc = lax.broadcasted_iota(I32, (K_TILE, K_TILE), 1)
    lower = jnp.where(lr >= lc, 1.0, 0.0).astype(BF16)

    def mask_tile(j, carry):
        s0 = pl.multiple_of(j * K_TILE, K_TILE)
        kt = key_scr[pl.ds(s0, K_TILE), :]
        eq = kt == theta
        eq_f = jnp.where(eq, 1.0, 0.0)
        prefix = carry + _mm(lower, eq_f.astype(BF16))
        take = ((kt > theta) | (eq & (prefix <= need))) & (s0 + key_pos < limit)
        key_scr[pl.ds(s0, K_TILE), :] = pltpu.bitcast(jnp.where(take, 0.0, NEG_BIG), I32)
        return carry + jnp.sum(eq_f, axis=0, keepdims=True)

    lax.fori_loop(0, n_tiles, mask_tile, jnp.zeros((1, Q_BLK), F32))

    m_scr[...] = jnp.full(m_scr.shape, NEG_BIG, F32)
    acc_scr[...] = jnp.zeros(acc_scr.shape, F32)
    p_scr[...] = jnp.zeros(p_scr.shape, BF16)
    alpha_scr[...] = jnp.ones(alpha_scr.shape, F32)

    def apply_pending(ct_t, ls):
        acc_scr[:, ls] = alpha_scr[:, ls] * acc_scr[:, ls] + _mm(ct_t, p_scr[:, ls])

    def attn_tile(j, carry, *, near):
        s0 = pl.multiple_of(j * K_TILE, K_TILE)
        c_t = c_ref[pl.ds(s0, K_TILE), :]
        ct_prev = ct_ref[:, pl.ds(pl.multiple_of(jnp.maximum(j - 1, 0) * K_TILE, K_TILE), K_TILE)]
        addm = pltpu.bitcast(key_scr[pl.ds(s0, K_TILE), :], F32)
        addm2 = jnp.concatenate([addm, addm], axis=1)

        def pair_logits(p):
            return _mm_nt(c_t, ql_ref[2 * p:2 * p + 2].reshape(2 * Q_BLK, KV_RANK))

        lg2_next = pair_logits(0)
        for p in range(H // 2):
            lg2 = lg2_next
            if p + 1 < H // 2:
                lg2_next = pair_logits(p + 1)
            ls = slice(2 * p * Q_BLK, (2 * p + 2) * Q_BLK)
            apply_pending(ct_prev, ls)
            lg = lg2 + addm2
            if near:
                lg = lg + jnp.concatenate([bt_ref[i - 2 * j, 2 * p], bt_ref[i - 2 * j, 2 * p + 1]], axis=1)
            m_prev = m_scr[:, ls]
            m_new = jnp.maximum(m_prev, jnp.max(lg, axis=0, keepdims=True))
            m_scr[:, ls] = m_new
            alpha_scr[:, ls] = jnp.exp2(m_prev - m_new)
            p_scr[:, ls] = jnp.exp2(lg - m_new).astype(BF16)
        return carry

    n_far = jnp.maximum(i - N_BIAS_TILES + 2, 0) // 2
    lax.fori_loop(0, n_far, functools.partial(attn_tile, near=False), 0)
    lax.fori_loop(n_far, n_tiles, functools.partial(attn_tile, near=True), 0)
    ct_last = ct_ref[:, pl.ds(pl.multiple_of((n_tiles - 1) * K_TILE, K_TILE), K_TILE)]
    for p in range(H // 2):
        apply_pending(ct_last, slice(2 * p * Q_BLK, (2 * p + 2) * Q_BLK))

    o_lat = (acc_scr[:KV_RANK, :] / acc_scr[KV_RANK:KV_RANK + 1, :]).T.astype(BF16)
    for p in range(H // 2):
        pair = jnp.concatenate([o_lat[2 * p * Q_BLK:(2 * p + 1) * Q_BLK], o_lat[(2 * p + 1) * Q_BLK:(2 * p + 2) * Q_BLK]],
                               axis=1)
        o_out[:, p * LANES:(p + 1) * LANES] = _mm(pair, wuv_ref[p])


ONES_ROWS = 16


def _dsa_operands(pp, c):
    ik = pp[:, 896:960]
    ik_hi = ik.astype(BF16)
    ik_lo = (ik - ik_hi.astype(F32)).astype(BF16)
    ik_packed = jnp.concatenate([ik_hi, ik_lo, ik_hi, jnp.zeros_like(ik_hi)], axis=1)
    iwt = jnp.transpose(pp[:, 960:968])
    ct_ext = jnp.concatenate([jnp.transpose(c), jnp.ones((ONES_ROWS, c.shape[0]), c.dtype)], axis=0)
    return iwt, ik_packed, c, ct_ext


def _dsa(pp, iwt, ik_packed, c, ct_ext, ql, bt, wuv2, *, topk):
    T = pp.shape[0]
    H = SA_HEADS
    assert T % (R_TILES * K_TILE) == 0 and N_BIAS_TILES * Q_BLK - (K_TILE - 1) >= _BUCKET_BOUNDS[-1]
    whole = pl.BlockSpec(memory_space=pltpu.VMEM)
    return pl.pallas_call(
        functools.partial(_dsa_kernel, topk=topk),
        grid=(T // Q_BLK,),
        in_specs=[pl.BlockSpec((Q_BLK, 512), lambda i: (i, 0)),
                  pl.BlockSpec((IDX_HEADS, Q_BLK), lambda i: (0, i)),
                  pl.BlockSpec((H, Q_BLK, KV_RANK), lambda i: (0, i, 0)),
                  whole, whole, whole, whole, whole],
        out_specs=pl.BlockSpec((Q_BLK, 1024), lambda i: (i, 0)),
        out_shape=jax.ShapeDtypeStruct((T, 1024), F32),
        scratch_shapes=[pltpu.VMEM((T, Q_BLK), I32),
                        pltpu.VMEM((IDX_HEADS // 2, 2 * Q_BLK, 2 * LANES), BF16),
                        pltpu.VMEM((1, H * Q_BLK), F32),
                        pltpu.VMEM((1, H * Q_BLK), F32),
                        pltpu.VMEM((K_TILE, H * Q_BLK), BF16),
                        pltpu.VMEM((KV_RANK + ONES_ROWS, H * Q_BLK), F32)],
        compiler_params=_cp(("parallel",), 56),
        name="dsa",
    )(pp, iwt, ql, ik_packed, c, ct_ext, bt, wuv2)


def _merge_kernel(y_ref, bonus_ref, g_ref, o_ref, ga_ref, gb_ref, x_ref, lnw_ref, lnb_ref, bd_ref,
                  wa_ref, wb_ref, wout_ref, h_out, m_scr):
    @pl.when(pl.program_id(1) == 0)
    def _():
        bd = bd_ref[...]

        def head_mean(x):
            hi, lo = _split2(x)
            return (_mm(hi, bd) + _mm(lo, bd)) * (1.0 / 64)

        y = y_ref[...]
        yc = y - head_mean(y)
        yn = yc * lax.rsqrt(head_mean(yc * yc) + RW_GN_EPS) * lnw_ref[...] + lnb_ref[...]
        ya = (yn + bonus_ref[...]) * g_ref[...]
        pa = _mm(ya.astype(BF16), wa_ref[...])
        pb = _mm(o_ref[...].astype(BF16), wb_ref[...])
        m = jax.nn.sigmoid(ga_ref[...]) * pa + jax.nn.sigmoid(gb_ref[...]) * pb
        m_scr[...] = m.astype(BF16)

    h_out[...] = x_ref[...] + _mm(m_scr[...], wout_ref[...])


def _merge(y, bonus, g, o, pm, x, ln_w, ln_b, bd, wa, wb, wout, *, tm, tn):
    T, D = x.shape
    W = RW_WIDTH
    row = lambda a: a.reshape(1, -1)
    act = pl.BlockSpec((tm, W), lambda i, j: (i, 0))
    full = lambda shape: pl.BlockSpec(shape, lambda i, j: (0,) * len(shape))
    return pl.pallas_call(
        _merge_kernel,
        grid=(T // tm, D // tn),
        in_specs=[act, act, act, act,
                  pl.BlockSpec((tm, D), lambda i, j: (i, 0)),
                  pl.BlockSpec((tm, D), lambda i, j: (i, 1)),
                  pl.BlockSpec((tm, tn), lambda i, j: (i, j)),
                  full((1, W)), full((1, W)), full((W, W)), full((W, D)), full((W, D)),
                  pl.BlockSpec((D, tn), lambda i, j: (0, j))],
        out_specs=pl.BlockSpec((tm, tn), lambda i, j: (i, j)),
        out_shape=jax.ShapeDtypeStruct((T, D), F32),
        scratch_shapes=[pltpu.VMEM((tm, D), BF16)],
        compiler_params=_cp(("parallel", "arbitrary"), 56),
        name="merge",
    )(y, bonus, g, o, pm, pm, x, row(ln_w), row(ln_b), bd, wa, wb, wout)


def _sort16_pairs():
    n, pairs, p = 16, [], 1
    while p < n:
        k = p
        while k >= 1:
            for j in range(k % p, n - k, 2 * k):
                for i in range(min(k, n - j - k)):
                    if (i + j) // (2 * p) == (i + j + k) // (2 * p):
                        pairs.append((i + j, i + j + k))
            k //= 2
        p *= 2
    return pairs


_SORT16 = _sort16_pairs()
PEER_MARGIN = 1.0 - 2.0 ** -20


def _exchange(a, i, j):
    a[i], a[j] = jnp.maximum(a[i], a[j]), jnp.minimum(a[i], a[j])


def _bitonic_sort16(a):
    for dist in (8, 4, 2, 1):
        for i in range(PEER_TOPK):
            if not i & dist:
                _exchange(a, i, i + dist)
    return a


def _top16_merge(x, y):
    return _bitonic_sort16([jnp.maximum(x[i], y[PEER_TOPK - 1 - i]) for i in range(PEER_TOPK)])


def _peer_select_kernel(qp_ref, keys_ref, et_out, tau_out):
    tm = qp_ref.shape[1]
    sub = lax.broadcasted_iota(I32, (SUBLANES, tm), 0)
    zero = jnp.zeros((SUBLANES, tm), F32)

    def per_head(h, packed):
        packed = list(packed)
        for c in range(2):
            hc = 2 * h + c
            s_t = _dot3(keys_ref[hc], qp_ref[hc], _mm_nt)
            e = jnp.exp(s_t - jnp.max(s_t, axis=0, keepdims=True))
            et_out[hc] = e
            a = [e[g * SUBLANES:(g + 1) * SUBLANES, :] for g in range(PEER_KEYS // SUBLANES)]
            for i, j in _SORT16:
                _exchange(a, i, j)
            for shift in (4, 2, 1):
                a = _top16_merge(a, [pltpu.roll(t, shift, 0) for t in a])
            for i in range(PEER_TOPK):
                packed[c * PEER_TOPK + i] = jnp.where(sub == h, a[i], packed[c * PEER_TOPK + i])
        return tuple(packed)

    packed = lax.fori_loop(0, PEER_HEADS, per_head, (zero,) * (2 * PEER_TOPK))
    top_a, top_b = packed[:PEER_TOPK], packed[PEER_TOPK:]

    def run(i, n):
        return [top_a[i] * top_b[j] for j in range(n)]

    g1 = run(0, 16)
    g2 = _bitonic_sort16(run(1, 8) + [top_a[i] * top_b[0] for i in range(PEER_TOPK - 1, 7, -1)])
    g3 = run(2, 5) + run(3, 4) + run(4, 3) + run(5, 2) + run(6, 2)
    for i, j in _SORT16:
        _exchange(g3, i, j)
    top = _top16_merge(_top16_merge(_top16_merge(g1, g2), g3), run(7, 2) + [zero] * 14)
    z = top[0]
    for t in top[1:]:
        z = z + t
    zinv = 1.0 / z
    tau_out[...] = top[PEER_TOPK - 1] * PEER_MARGIN * zinv
    for h in range(PEER_HEADS):
        et_out[2 * h] = et_out[2 * h] * zinv[h:h + 1, :]


def _peer_select(qp, keys, *, tm):
    T = qp.shape[1]
    return pl.pallas_call(
        _peer_select_kernel,
        grid=(T // tm,),
        in_specs=[pl.BlockSpec((2 * PEER_HEADS, tm, LANES), lambda i: (0, i, 0)),
                  pl.BlockSpec((2 * PEER_HEADS, PEER_KEYS, LANES), lambda i: (0, 0, 0))],
        out_specs=[pl.BlockSpec((2 * PEER_HEADS, PEER_KEYS, tm), lambda i: (0, 0, i)),
                   pl.BlockSpec((SUBLANES, tm), lambda i: (0, i))],
        out_shape=[jax.ShapeDtypeStruct((2 * PEER_HEADS, PEER_KEYS, T), F32),
                   jax.ShapeDtypeStruct((PEER_HEADS, T), F32)],
        compiler_params=_cp(("parallel",)),
        name="peer_select",
    )(qp, keys)


def _peer_kernel(h_ref, gf_ref, gout_ref, u_ref, vt_ref, et_ref, tau_ref, out_ref, hn_scr, acc_scr, ga0_scr, ga1_scr,
                 *, e_tile):
    j = pl.program_id(1)
    n_tiles = pl.num_programs(1) - 1
    tm = h_ref.shape[0]
    per_tile = e_tile // PEER_KEYS

    @pl.when(j == 0)
    def _():
        h = h_ref[...]
        hn = h * lax.rsqrt(jnp.mean(h * h, axis=-1, keepdims=True) + EPS) * gf_ref[...]
        hn_scr[...] = hn.astype(BF16)
        acc_scr[...] = jnp.zeros_like(acc_scr)
        ga1_scr[...] = jnp.zeros_like(ga1_scr)

    d_model = vt_ref.shape[0]
    half = e_tile // 2

    def pre_act(s):
        return _mm_nt(u_ref[s * half:(s + 1) * half, :], hn_scr[...])

    def apply_values(q, ga_read):
        rows = slice(q * (d_model // 8), (q + 1) * (d_model // 8))
        acc_scr[rows, :] += _mm(vt_ref[rows, :], ga_read[...])

    def gated(ai, pre, ga_write):
        rows = slice((ai % 2) * PEER_KEYS, (ai % 2 + 1) * PEER_KEYS)
        x = pre[rows, :]
        act = 0.5 * x * (1.0 + lax.erf(x * (2.0 ** -0.5)))
        a = jnp.minimum(j, n_tiles - 1) * per_tile + ai
        gate = jnp.zeros((PEER_KEYS, tm), F32)
        for h in range(PEER_HEADS):
            w = et_ref[2 * h + 1] * et_ref[2 * h, pl.ds(a, 1), :]
            gate = gate + jnp.where(w >= tau_ref[h:h + 1, :], w, 0.0)
        ga_write[ai * PEER_KEYS:(ai + 1) * PEER_KEYS, :] = (gate * act).astype(BF16)

    def step(ga_write, ga_read):
        pre0 = pre_act(0)
        apply_values(0, ga_read)
        apply_values(1, ga_read)
        gated(0, pre0, ga_write)
        pre1 = pre_act(1)
        gated(1, pre0, ga_write)
        apply_values(2, ga_read)
        apply_values(3, ga_read)
        apply_values(4, ga_read)
        gated(2, pre1, ga_write)
        apply_values(5, ga_read)
        apply_values(6, ga_read)
        apply_values(7, ga_read)
        gated(3, pre1, ga_write)

    @pl.when(j % 2 == 0)
    def _():
        step(ga0_scr, ga1_scr)

    @pl.when(j % 2 == 1)
    def _():
        step(ga1_scr, ga0_scr)

    @pl.when(j == n_tiles)
    def _():
        h3 = h_ref[...] + acc_scr[...].T
        out_ref[...] = h3 * lax.rsqrt(jnp.mean(h3 * h3, axis=-1, keepdims=True) + EPS) * gout_ref[...]


def _peer(h1, g_ffn, g_out, u, vt, et, tau, *, tm, e_tile):
    T, D = h1.shape
    E = u.shape[0]
    n_tiles = E // e_tile
    assert n_tiles % 2 == 0
    return pl.pallas_call(
        functools.partial(_peer_kernel, e_tile=e_tile),
        grid=(T // tm, n_tiles + 1),
        in_specs=[pl.BlockSpec((tm, D), lambda i, j: (i, 0)),
                  pl.BlockSpec((1, D), lambda i, j: (0, 0)),
                  pl.BlockSpec((1, D), lambda i, j: (0, 0)),
                  pl.BlockSpec((e_tile, D), lambda i, j: (jnp.minimum(j, n_tiles - 1), 0)),
                  pl.BlockSpec((D, e_tile), lambda i, j: (0, jnp.maximum(j - 1, 0))),
                  pl.BlockSpec((2 * PEER_HEADS, PEER_KEYS, tm), lambda i, j: (0, 0, i)),
                  pl.BlockSpec((PEER_HEADS, tm), lambda i, j: (0, i))],
        out_specs=pl.BlockSpec((tm, D), lambda i, j: (i, 0)),
        out_shape=jax.ShapeDtypeStruct((T, D), F32),
        scratch_shapes=[pltpu.VMEM((tm, D), BF16), pltpu.VMEM((D, tm), F32),
                        pltpu.VMEM((e_tile, tm), BF16), pltpu.VMEM((e_tile, tm), BF16)],
        compiler_params=_cp(("parallel", "arbitrary"), 56),
        name="peer",
    )(h1, g_ffn.reshape(1, D), g_out.reshape(1, D), u, vt, et, tau)


def _pad_cols(a, n):
    return jnp.pad(a, ((0, 0), (0, n - a.shape[1])))


def _block_diag2(a, b):
    za = jnp.zeros((a.shape[0], b.shape[1]), a.dtype)
    zb = jnp.zeros((b.shape[0], a.shape[1]), a.dtype)
    return jnp.concatenate([jnp.concatenate([a, za], axis=1), jnp.concatenate([zb, b], axis=1)], axis=0)


def kernel(x, norm_mix_g, w_in, shift_mix, w0, w_decay_up, a0, w_icl_up, w_gate_up, k_k, k_a, r_k, ln_x_w, ln_x_b,
           kv_norm_g, w_uk, w_uv, rel_bias, w_branch_rwkv, w_branch_dsa, w_out, norm_ffn_g, w_peer_query,
           peer_sub_keys, peer_u, peer_v, norm_final_g):
    B, T, D = x.shape
    assert B == 1 and D == D_MODEL and T % (2 * K_TILE) == 0 and norm_mix_g.shape[0] == 1
    topk = min(TOPK_MAX, T // 4)
    x2 = x[0]
    w = w_in[0]
    sm = shift_mix[0]
    o_zw, o_za, o_zg, o_q, o_ckv, o_iq, o_ik, o_iw, o_g = 3072, 3136, 3200, 3360, 4384, 4640, 5152, 5216, 5224
    w_main = jnp.concatenate([w[:, o_g:o_g + 4096], w[:, :3072], w[:, o_q:o_q + 1024], w[:, o_ckv:o_ckv + 256]],
                             axis=1).astype(BF16)
    w_prec = jnp.concatenate([w[:, o_iq:o_iq + 512], _pad_cols(w[:, o_zg:o_zg + 160], 256), w[:, o_zw:o_zw + 128],
                              _pad_cols(w[:, o_ik:o_ik + 72], 128)], axis=1)
    tm = min(512, T)
    pm = _proj(x2, norm_mix_g[0], w_main, precise=False, tm=tm, tn=768, name="proj_main")
    pp = _proj(x2, norm_mix_g[0], w_prec, precise=True, tm=tm, tn=512, name="proj_precise")

    head_id = jnp.arange(RW_WIDTH) // 64
    bd = (head_id[:, None] == head_id[None, :]).astype(BF16)
    mixes = [sm[:1024], sm[1024:2048], sm[2048:3072], sm[o_zw:o_zw + 128],
             jnp.pad(sm[o_zg:o_zg + 160], (0, 96))]
    wd = jnp.pad(w_decay_up[0], ((0, 64), (0, 0)))
    wi = jnp.pad(w_icl_up[0], ((64, 0), (0, 0)))
    wg = jnp.pad(w_gate_up[0], ((0, 96), (0, 0)))
    r, k4, v, kkn, b, lw, g, bonus = _rw_prep(pm, pp, mixes, w0[0], wd, a0[0], wi, wg, k_k[0], k_a[0],
                                              r_k[0].reshape(-1), bd, tm=min(256, T))
    nch = T // RW_CHUNK
    p_mat, q_mat, rp, y0 = _rw_chunk(r, k4, v, kkn, b, lw, G=min(8, nch))
    y = _rw_scan(p_mat, q_mat, rp, y0)

    wuk_t = jnp.transpose(w_uk[0], (0, 2, 1))
    wuk2 = jnp.stack([_block_diag2(wuk_t[2 * p], wuk_t[2 * p + 1]) for p in range(SA_HEADS // 2)]).astype(BF16)
    wuv2 = jnp.stack([_block_diag2(w_uv[0][2 * p], w_uv[0][2 * p + 1]) for p in range(SA_HEADS // 2)]).astype(BF16)
    c, ql = _dsa_prep(pm, kv_norm_g[0], wuk2, tm=min(512, T))
    o_dsa = _dsa(pp, *_dsa_operands(pp, c), ql, _rel_bias_tiles(rel_bias), wuv2, topk=topk)

    h1 = _merge(y, bonus, g, o_dsa, pm, x2, ln_x_w[0], ln_x_b[0], bd,
                w_branch_rwkv[0].astype(BF16), w_branch_dsa[0].astype(BF16), w_out[0].astype(BF16),
                tm=min(256, T), tn=512)

    qp = _proj(h1, norm_ffn_g[0], w_peer_query[0], precise=True, tm=tm, tn=512, name="peer_query", lane_groups=True)
    keys = peer_sub_keys[0].reshape(2 * PEER_HEADS, PEER_KEYS, LANES)
    et, tau = _peer_select(qp, keys, tm=128)
    u = peer_u[0].astype(BF16)
    vt = jnp.transpose(peer_v[0]).astype(BF16)
    out = _peer(h1, norm_ffn_g[0], norm_final_g, u, vt, et, tau, tm=min(512, T), e_tile=512)
    return out[None]
```

```python
import functools
import math

import numpy as np
import jax
import jax.numpy as jnp
from jax import lax
from jax.experimental import pallas as pl
from jax.experimental.pallas import tpu as pltpu

F32, BF16, I32 = jnp.float32, jnp.bfloat16, jnp.int32

D_MODEL = 2048
EPS = 1e-6
RW_WIDTH = 1024
RW_GN_EPS = 64e-5
RW_CHUNK = 128
RW_LOCKSTEP = 4
SA_HEADS = 16
SA_SCALE = 64 ** -0.5
KV_RANK = 256
IDX_HEADS = 8
IDX_SCALE = (8 * 64) ** -0.5
TOPK_MAX = 256
REL_BUCKETS = 32
REL_MAX_DIST = 1024
PEER_HEADS = 8
PEER_KEYS = 128
PEER_TOPK = 16
LANES = 128
SUBLANES = 8
NEG_BIG = -1e30
INT_MIN = -2 ** 31

Q_BLK = 128
K_TILE = 256
N_BIAS_TILES = 7
R_TILES = 4
LOG2E = math.log2(math.e)


def _cp(sem, vmem_mb=48):
    return pltpu.CompilerParams(dimension_semantics=sem, vmem_limit_bytes=vmem_mb * 1024 * 1024)


def _bf(x):
    return x.astype(BF16)


def _split2(x):
    hi = x.astype(BF16)
    lo = (x - hi.astype(F32)).astype(BF16)
    return hi, lo


def _mm(a, b):
    return jnp.dot(a, b, preferred_element_type=F32)


def _mm_nt(a, b):
    return lax.dot_general(a, b, (((1,), (1,)), ((), ())), preferred_element_type=F32)


def _mm_tn(a, b):
    return lax.dot_general(a, b, (((0,), (0,)), ((), ())), preferred_element_type=F32)


def _d3(a_s, b_s, mm=_mm):
    return mm(a_s[0], b_s[0]) + (mm(a_s[0], b_s[1]) + mm(a_s[1], b_s[0]))


def _dot3(a, b, mm=_mm):
    return _d3(_split2(a), _split2(b), mm)


def _proj_kernel(x_ref, g_ref, w_ref, o_ref, xh_ref, xl_ref, *, precise, lane_groups):
    @pl.when(pl.program_id(1) == 0)
    def _():
        x = x_ref[...]
        xn = x * lax.rsqrt(jnp.mean(x * x, axis=-1, keepdims=True) + EPS) * g_ref[...]
        hi = xn.astype(BF16)
        xh_ref[...] = hi
        if precise:
            xl_ref[...] = (xn - hi.astype(F32)).astype(BF16)

    if precise:
        wh, wl = _split2(w_ref[...])
        xh = xh_ref[...]
        res = _mm(xh, wh) + (_mm(xh, wl) + _mm(xl_ref[...], wh))
    else:
        res = _mm(xh_ref[...], w_ref[...])
    if lane_groups:
        for k in range(res.shape[1] // LANES):
            o_ref[k] = res[:, k * LANES:(k + 1) * LANES]
    else:
        o_ref[...] = res


def _proj(x, g, w, *, precise, tm, tn, name, lane_groups=False):
    T, D = x.shape
    N = w.shape[1]
    lo_rows = tm if precise else SUBLANES * 2
    if lane_groups:
        out_spec = pl.BlockSpec((tn // LANES, tm, LANES), lambda i, j: (j, i, 0))
        out_shape = jax.ShapeDtypeStruct((N // LANES, T, LANES), F32)
    else:
        out_spec = pl.BlockSpec((tm, tn), lambda i, j: (i, j))
        out_shape = jax.ShapeDtypeStruct((T, N), F32)
    return pl.pallas_call(
        functools.partial(_proj_kernel, precise=precise, lane_groups=lane_groups),
        grid=(T // tm, N // tn),
        in_specs=[pl.BlockSpec((tm, D), lambda i, j: (i, 0)),
                  pl.BlockSpec((1, D), lambda i, j: (0, 0)),
                  pl.BlockSpec((D, tn), lambda i, j: (0, j))],
        out_specs=out_spec,
        out_shape=out_shape,
        scratch_shapes=[pltpu.VMEM((tm, D), BF16), pltpu.VMEM((lo_rows, D), BF16)],
        compiler_params=_cp(("parallel", "arbitrary")),
        name=name,
    )(x, g.reshape(1, D), w)


def _rw_prep_kernel(r_ref, k_ref, v_ref, wa_ref, zg_ref, rp_ref, kp_ref, vp_ref, wap_ref, zgp_ref,
                    mr_ref, mk_ref, mv_ref, mwa_ref, mg_ref, w0_ref, wd_ref, a0_ref, wi_ref, wg_ref,
                    kk_ref, ka_ref, rk_ref, bd_ref,
                    r_out, k_out, v_out, kkn_out, b_out, lw_out, g_out, bonus_out):
    first = pl.program_id(0) == 0

    def shift(p_ref, prev_ref, mix_ref):
        p = p_ref[...]
        last = jnp.where(first, 0.0, prev_ref[SUBLANES - 1:SUBLANES, :])
        row = lax.broadcasted_iota(I32, p.shape, 0)
        prev = jnp.where(row == 0, last, pltpu.roll(p, 1, 0))
        return p + (prev - p) * mix_ref[...]

    r = shift(r_ref, rp_ref, mr_ref)
    k = shift(k_ref, kp_ref, mk_ref)
    v = shift(v_ref, vp_ref, mv_ref)
    zwa = shift(wa_ref, wap_ref, mwa_ref)
    zg = shift(zg_ref, zgp_ref, mg_ref)

    nd = -(w0_ref[...] + _dot3(jnp.tanh(zwa), wd_ref[...]))
    softplus = jnp.maximum(nd, 0.0) + jnp.log1p(jnp.exp(-jnp.abs(nd)))
    lw = -jnp.exp(-softplus - 0.5)
    a = jax.nn.sigmoid(a0_ref[...] + _dot3(zwa, wi_ref[...]))
    g = _dot3(jax.nn.sigmoid(zg), wg_ref[...])

    bd = bd_ref[...]
    wide = bd.shape[0]

    def head_sum(x):
        hi, lo = _split2(x)
        parts = [_mm(hi[:, p * wide:(p + 1) * wide], bd) + _mm(lo[:, p * wide:(p + 1) * wide], bd)
                 for p in range(RW_WIDTH // wide)]
        return jnp.concatenate(parts, axis=1)

    kk = k * kk_ref[...]
    kkn = kk / jnp.maximum(jnp.sqrt(head_sum(kk * kk)), 1e-12)
    k4 = k * (1.0 + (a - 1.0) * ka_ref[...])
    r_out[...] = r
    k_out[...] = k4
    v_out[...] = v
    kkn_out[...] = kkn
    b_out[...] = kkn * a
    lw_out[...] = lw
    g_out[...] = g
    bonus_out[...] = head_sum(r * k4 * rk_ref[...]) * v


def _rw_prep(pm, pp, mixes, w0, wd, a0, wi, wg, k_k, k_a, r_k, bd, *, tm):
    T = pm.shape[0]
    W = RW_WIDTH
    pb = tm // SUBLANES

    def cur(width, col):
        return pl.BlockSpec((tm, width), lambda i: (i, col))

    def prev(width, col):
        return pl.BlockSpec((SUBLANES, width), lambda i: (jnp.maximum(i * pb - 1, 0), col))

    def full(shape):
        return pl.BlockSpec(shape, lambda i: (0,) * len(shape))

    row = lambda a: a.reshape(1, -1)
    out_sd = jax.ShapeDtypeStruct((T, W), F32)
    return pl.pallas_call(
        _rw_prep_kernel,
        grid=(T // tm,),
        in_specs=[cur(W, 4), cur(W, 5), cur(W, 6), cur(128, 6), cur(256, 2),
                  prev(W, 4), prev(W, 5), prev(W, 6), prev(128, 6), prev(256, 2),
                  full((1, W)), full((1, W)), full((1, W)), full((1, 128)), full((1, 256)),
                  full((1, W)), full((128, W)), full((1, W)), full((128, W)), full((256, W)),
                  full((1, W)), full((1, W)), full((1, W)), full((2 * LANES, 2 * LANES))],
        out_specs=[pl.BlockSpec((tm, W), lambda i: (i, 0))] * 8,
        out_shape=[out_sd] * 8,
        compiler_params=_cp(("parallel",)),
        name="rw_prep",
    )(pm, pm, pm, pp, pp, pm, pm, pm, pp, pp,
      *[row(m) for m in mixes], row(w0), wd, row(a0), wi, wg, row(k_k), row(k_a), row(r_k), bd[:2 * LANES, :2 * LANES])


def _rw_chunk_kernel(r_ref, k_ref, v_ref, kkn_ref, b_ref, lw_ref, p_out, q_out, rp_out, y0_out, *, G):
    C = RW_CHUNK
    row = lax.broadcasted_iota(I32, (C, C), 0)
    col = lax.broadcasted_iota(I32, (C, C), 1)
    incl = row >= col
    strict = row > col
    eye = row == col
    same_head = (row // 64) == (col // 64)
    tri = jnp.where(incl, 1.0, 0.0).astype(BF16)
    lane = lax.broadcasted_iota(I32, (C, LANES), 1)
    eye_f = jnp.where(eye, 1.0, 0.0)

    head_lanes = (lane < 64, lane >= 64)
    U = RW_LOCKSTEP

    def body(cu, carry):
        ch = [cu * U + u for u in range(U)]
        sls = [pl.ds(pl.multiple_of(c * C, C), C) for c in ch]
        pre = []
        for sl in sls:
            r, k4, v, kkn, b, lw = (ref[sl, :] for ref in (r_ref, k_ref, v_ref, kkn_ref, b_ref, lw_ref))
            h1 = lw.astype(BF16)
            r1 = lw - h1.astype(F32)
            h2 = r1.astype(BF16)
            h3 = (r1 - h2.astype(F32)).astype(BF16)
            L = _mm(tri, h1) + (_mm(tri, h2) + _mm(tri, h3))
            Ltot = L[C - 1:C, :]
            Lr = L - L[C // 2 - 1:C // 2, :]
            eneg = jnp.exp(-Lr)
            tail = jnp.exp(Ltot - L)
            pre.append(dict(
                rhat=r * jnp.exp(Lr), ahat=-kkn * jnp.exp(Lr - lw), kt=_bf(k4 * eneg), bt=_bf(b * eneg),
                rabs=r * jnp.exp(L), aabs=-kkn * jnp.exp(L - lw), kbar=_bf(k4 * tail), bbar=_bf(b * tail),
                gam=jnp.exp(Ltot), vb=_bf(v)))
        inst = [(u, hh) for u in range(U) for hh in range(2)]
        am = [_bf(jnp.where(head_lanes[hh], pre[u]["ahat"], 0.0)) for u, hh in inst]
        rm = [_bf(jnp.where(head_lanes[hh], pre[u]["rhat"], 0.0)) for u, hh in inst]
        a_ab = [jnp.where(strict, _mm_nt(am[k], pre[u]["bt"]), 0.0) for k, (u, hh) in enumerate(inst)]
        a_ak = [_bf(jnp.where(strict, _mm_nt(am[k], pre[u]["kt"]), 0.0)) for k, (u, hh) in enumerate(inst)]
        m_rb = [_bf(jnp.where(incl, _mm_nt(rm[k], pre[u]["bt"]), 0.0)) for k, (u, hh) in enumerate(inst)]
        m_rk = [_bf(jnp.where(incl, _mm_nt(rm[k], pre[u]["kt"]), 0.0)) for k, (u, hh) in enumerate(inst)]
        n = a_ab
        x = [eye_f + t for t in n]
        for _ in range(6):
            nb = [_bf(t) for t in n]
            n = [_mm(t, t) for t in nb]
            x = [xi + _mm(_bf(xi), _bf(ni)) for xi, ni in zip(x, n)]
        xb = [_bf(t) for t in x]
        akv = [_bf(_mm(a_ak[k], pre[u]["vb"])) for k, (u, hh) in enumerate(inst)]
        ah_h = [_mm(xb[k], _bf(jnp.where(head_lanes[hh], pre[u]["aabs"], 0.0))) for k, (u, hh) in enumerate(inst)]
        u0_h = [_mm(xb[k], akv[k]) for k in range(len(inst))]
        y0_h = [_mm(m_rk[k], pre[u]["vb"]) + _mm(m_rb[k], _bf(u0_h[k])) for k, (u, hh) in enumerate(inst)]
        rp_h = [jnp.where(head_lanes[hh], pre[u]["rabs"], 0.0) + _mm(m_rb[k], _bf(ah_h[k]))
                for k, (u, hh) in enumerate(inst)]
        for u in range(U):
            lo, hi = 2 * u, 2 * u + 1
            ah = _bf(ah_h[lo] + ah_h[hi])
            u0 = _bf(jnp.where(head_lanes[0], u0_h[lo], u0_h[hi]))
            pu = pre[u]
            p_out[0, ch[u]] = jnp.where(eye, pu["gam"], 0.0) + jnp.where(same_head, _mm_tn(pu["bbar"], ah), 0.0)
            q_out[0, ch[u]] = jnp.where(same_head, _mm_tn(pu["kbar"], pu["vb"]) + _mm_tn(pu["bbar"], u0), 0.0)
            rp_out[sls[u], :] = rp_h[lo] + rp_h[hi]
            y0_out[sls[u], :] = jnp.where(head_lanes[0], y0_h[lo], y0_h[hi])
        return carry

    lax.fori_loop(0, G // U, body, 0)


def _rw_chunk(r, k4, v, kkn, b, lw, *, G):
    T = r.shape[0]
    C = RW_CHUNK
    nch = T // C
    rows = G * C
    cur = pl.BlockSpec((rows, LANES), lambda p, g: (g, p))
    mat = pl.BlockSpec((1, G, C, C), lambda p, g: (p, g, 0, 0))
    return pl.pallas_call(
        functools.partial(_rw_chunk_kernel, G=G),
        grid=(RW_WIDTH // LANES, nch // G),
        in_specs=[cur] * 6,
        out_specs=[mat, mat, cur, cur],
        out_shape=[jax.ShapeDtypeStruct((RW_WIDTH // LANES, nch, C, C), F32)] * 2
        + [jax.ShapeDtypeStruct((T, RW_WIDTH), F32)] * 2,
        compiler_params=_cp(("parallel", "parallel")),
        name="rw_chunk",
    )(r, k4, v, kkn, b, lw)


def _rw_scan_kernel(p_ref, q_ref, rp_ref, y0_ref, y_out, s_scr):
    @pl.when(pl.program_id(0) == 0)
    def _():
        s_scr[...] = jnp.zeros_like(s_scr)

    for p in range(RW_WIDTH // LANES):
        ls = slice(p * LANES, (p + 1) * LANES)
        s0 = _bf(s_scr[p])
        y_out[:, ls] = _mm(_bf(rp_ref[:, ls]), s0) + y0_ref[:, ls]
        s_scr[p] = _mm(_bf(p_ref[p, 0]), s0) + q_ref[p, 0]


def _rw_scan(p_mat, q_mat, rp, y0):
    T = rp.shape[0]
    C = RW_CHUNK
    npair = RW_WIDTH // LANES
    mat = pl.BlockSpec((npair, 1, C, C), lambda c: (0, c, 0, 0))
    cur = pl.BlockSpec((C, RW_WIDTH), lambda c: (c, 0))
    return pl.pallas_call(
        _rw_scan_kernel,
        grid=(T // C,),
        in_specs=[mat, mat, cur, cur],
        out_specs=cur,
        out_shape=jax.ShapeDtypeStruct((T, RW_WIDTH), F32),
        scratch_shapes=[pltpu.VMEM((npair, C, C), F32)],
        compiler_params=_cp(("arbitrary",)),
        name="rw_scan",
    )(p_mat, q_mat, rp, y0)


def _dsa_prep_kernel(q_ref, ckv_ref, g_ref, wuk_ref, c_out, ql_out):
    ckv = ckv_ref[...]
    c = ckv * lax.rsqrt(jnp.mean(ckv * ckv, axis=-1, keepdims=True) + EPS) * g_ref[...]
    c_out[...] = c.astype(BF16)
    for p in range(SA_HEADS // 2):
        qp = q_ref[:, p * LANES:(p + 1) * LANES].astype(BF16)
        ql = _mm(qp, wuk_ref[p]) * (SA_SCALE * LOG2E)
        ql_out[2 * p] = ql[:, :KV_RANK].astype(BF16)
        ql_out[2 * p + 1] = ql[:, KV_RANK:].astype(BF16)


def _dsa_prep(pm, kv_norm_g, wuk2, *, tm):
    T = pm.shape[0]
    return pl.pallas_call(
        _dsa_prep_kernel,
        grid=(T // tm,),
        in_specs=[pl.BlockSpec((tm, 1024), lambda i: (i, 7)),
                  pl.BlockSpec((tm, KV_RANK), lambda i: (i, 32)),
                  pl.BlockSpec((1, KV_RANK), lambda i: (0, 0)),
                  pl.BlockSpec((SA_HEADS // 2, LANES, 2 * KV_RANK), lambda i: (0, 0, 0))],
        out_specs=[pl.BlockSpec((tm, KV_RANK), lambda i: (i, 0)),
                   pl.BlockSpec((SA_HEADS, tm, KV_RANK), lambda i: (0, i, 0))],
        out_shape=[jax.ShapeDtypeStruct((T, KV_RANK), BF16),
                   jax.ShapeDtypeStruct((SA_HEADS, T, KV_RANK), BF16)],
        compiler_params=_cp(("parallel",)),
        name="dsa_prep",
    )(pm, pm, kv_norm_g.reshape(1, KV_RANK), wuk2)


def _bucket_bounds():
    nb = REL_BUCKETS // 2
    max_exact = nb // 2
    n = np.arange(1, 2 * REL_MAX_DIST, dtype=np.int64)
    large = max_exact + (np.log(n.astype(np.float32) / max_exact)
                         / math.log(REL_MAX_DIST / max_exact) * (nb - max_exact)).astype(np.int32)
    large = np.minimum(large, nb - 1)
    bounds = [int(n[np.argmax(large >= b)]) for b in range(max_exact + 1, nb)]
    assert all(b2 > b1 for b1, b2 in zip(bounds, bounds[1:]))
    return bounds


_BUCKET_BOUNDS = _bucket_bounds()


def _rel_bias_kernel(rb_ref, bt_out):
    o = pl.program_id(0)
    sl = lax.broadcasted_iota(I32, (K_TILE, Q_BLK), 0)
    tl = lax.broadcasted_iota(I32, (K_TILE, Q_BLK), 1)
    rel = sl - tl - o * Q_BLK
    nb = REL_BUCKETS // 2
    max_exact = nb // 2
    n = jnp.abs(rel)
    large = jnp.full(rel.shape, max_exact, I32)
    for bound in _BUCKET_BOUNDS:
        large = large + jnp.where(n >= bound, 1, 0)
    bucket = jnp.where(rel > 0, nb, 0) + jnp.where(n < max_exact, n, large)
    for h in range(SA_HEADS):
        acc = jnp.zeros(rel.shape, F32)
        for bkt in range(REL_BUCKETS):
            acc = jnp.where(bucket == bkt, rb_ref[bkt, h], acc)
        bt_out[0, h] = (acc - rb_ref[nb - 1, h]) * LOG2E


def _rel_bias_tiles(rel_bias):
    return pl.pallas_call(
        _rel_bias_kernel,
        grid=(N_BIAS_TILES,),
        in_specs=[pl.BlockSpec(memory_space=pltpu.SMEM)],
        out_specs=pl.BlockSpec((1, SA_HEADS, K_TILE, Q_BLK), lambda o: (o, 0, 0, 0)),
        out_shape=jax.ShapeDtypeStruct((N_BIAS_TILES, SA_HEADS, K_TILE, Q_BLK), F32),
        compiler_params=_cp(("parallel",)),
        name="rel_bias",
    )(rel_bias)


def _dsa_kernel(iq_ref, iwt_ref, ql_ref, ik_ref, c_ref, ct_ref, bt_ref, wuv_ref, o_out,
                key_scr, iq_scr, m_scr, alpha_scr, p_scr, acc_scr, *, topk):
    i = pl.program_id(0)
    t0 = i * Q_BLK
    n_tiles = i // 2 + 1
    H = SA_HEADS
    lane_q = lax.broadcasted_iota(I32, (Q_BLK, LANES), 1)
    first = lane_q < 64

    for p in range(IDX_HEADS // 2):
        pair = iq_ref[:, p * LANES:(p + 1) * LANES]
        swap = pltpu.roll(pair, 64, 1)
        rows = []
        for own, other in ((pair, swap), (swap, pair)):
            o_hi, o_lo = _split2(own)
            t_hi = other.astype(BF16)
            rows.append(jnp.concatenate([jnp.where(first, o_hi, t_hi), jnp.where(first, o_lo, jnp.zeros_like(o_lo))], axis=1))
        iq_scr[p] = jnp.concatenate(rows, axis=0)
    w_rows = [iwt_ref[h:h + 1, :] * IDX_SCALE for h in range(IDX_HEADS)]

    key_pos = lax.broadcasted_iota(I32, (K_TILE, Q_BLK), 0)
    q_lane = lax.broadcasted_iota(I32, (K_TILE, Q_BLK), 1)
    limit = t0 + (q_lane // 64 + 1) * 64

    def score_tiles(jj, carry):
        starts = [pl.multiple_of((2 * jj + u) * K_TILE, K_TILE) for u in range(2)]
        dots = [[_mm_nt(ik_ref[pl.ds(s0, K_TILE), :], iq_scr[p]) for p in range(IDX_HEADS // 2)] for s0 in starts]
        for s0, tile_dots in zip(starts, dots):
            sc = jnp.zeros((K_TILE, Q_BLK), F32)
            for p, d in enumerate(tile_dots):
                sc = (sc + jnp.maximum(d[:, :Q_BLK], 0.0) * w_rows[2 * p]
                      + jnp.maximum(d[:, Q_BLK:], 0.0) * w_rows[2 * p + 1])
            sc = sc + 0.0
            bits = pltpu.bitcast(sc, I32)
            key = bits ^ ((bits >> 31) & 0x7FFFFFFF)
            key_scr[pl.ds(s0, K_TILE), :] = jnp.where(s0 + key_pos < limit, key, INT_MIN)
        return carry

    lax.fori_loop(0, (n_tiles + 1) // 2, score_tiles, 0)

    n_steps = (n_tiles + R_TILES - 1) // R_TILES

    def pad_tile(j, carry):
        key_scr[pl.ds(pl.multiple_of(j * K_TILE, K_TILE), K_TILE), :] = jnp.full((K_TILE, Q_BLK), INT_MIN, I32)
        return carry

    lax.fori_loop(n_tiles, n_steps * R_TILES, pad_tile, 0)

    def count(pred):
        def step(js, acc):
            s0 = pl.multiple_of(js * (R_TILES * K_TILE), R_TILES * K_TILE)
            for sub in range(R_TILES * K_TILE // 64):
                hit = jnp.where(pred(key_scr[pl.ds(s0 + sub * 64, 64), :]), 1.0, 0.0)
                acc = acc + jnp.sum(hit.reshape(64 // SUBLANES, SUBLANES, Q_BLK), axis=0)
            return acc
        acc = lax.fori_loop(0, n_steps, step, jnp.zeros((SUBLANES, Q_BLK), F32))
        return jnp.sum(acc, axis=0, keepdims=True)

    def bit_step(it, cur):
        bit = lax.shift_left(jnp.int32(1), 31 - it)
        cand = (cur | bit) ^ INT_MIN
        cnt = count(lambda kt: kt >= cand)
        return jnp.where(cnt >= topk, cur | bit, cur)

    cur = lax.fori_loop(0, 32, bit_step, jnp.zeros((1, Q_BLK), I32))
    theta = cur ^ INT_MIN
    need = jnp.where(theta == INT_MIN, 0.0, topk - count(lambda kt: kt > theta))

    lr = lax.broadcasted_iota(I32, (K_TILE, K_TILE), 0)
    lc = lax.broadcasted_iota(I32, (K_TILE, K_TILE), 1)
    lower = jnp.where(lr >= lc, 1.0, 0.0).astype(BF16)

    def mask_tiles(jj, carry):
        starts = [pl.multiple_of((2 * jj + u) * K_TILE, K_TILE) for u in range(2)]
        kts = [key_scr[pl.ds(s0, K_TILE), :] for s0 in starts]
        eqs = [jnp.where(kt == theta, 1.0, 0.0) for kt in kts]
        within = [_mm(lower, e.astype(BF16)) for e in eqs]
        for s0, kt, e, w in zip(starts, kts, eqs, within):
            tie = jnp.where(carry + w <= need, 0.0, NEG_BIG)
            addm = jnp.where(kt > theta, 0.0, jnp.where(kt == theta, tie, NEG_BIG))
            key_scr[pl.ds(s0, K_TILE), :] = pltpu.bitcast(addm, I32)
            carry = carry + jnp.sum(e, axis=0, keepdims=True)
        return carry

    lax.fori_loop(0, (n_tiles + 1) // 2, mask_tiles, jnp.zeros((1, Q_BLK), F32))

    m_scr[...] = jnp.full(m_scr.shape, NEG_BIG, F32)
    acc_scr[...] = jnp.zeros(acc_scr.shape, F32)
    p_scr[...] = jnp.zeros(p_scr.shape, BF16)
    alpha_scr[...] = jnp.ones(alpha_scr.shape, F32)

    def apply_pending(ct_t, ls):
        acc_scr[:, ls] = alpha_scr[:, ls] * acc_scr[:, ls] + _mm(ct_t, p_scr[:, ls])

    def attn_tile(j, carry, *, near):
        s0 = pl.multiple_of(j * K_TILE, K_TILE)
        c_t = c_ref[pl.ds(s0, K_TILE), :]
        ct_prev = ct_ref[:, pl.ds(pl.multiple_of(jnp.maximum(j - 1, 0) * K_TILE, K_TILE), K_TILE)]
        addm = pltpu.bitcast(key_scr[pl.ds(s0, K_TILE), :], F32)
        addm2 = jnp.concatenate([addm, addm], axis=1)

        def pair_logits(p):
            return _mm_nt(c_t, ql_ref[2 * p:2 * p + 2].reshape(2 * Q_BLK, KV_RANK))

        lg2_next = pair_logits(0)
        for p in range(H // 2):
            lg2 = lg2_next
            if p + 1 < H // 2:
                lg2_next = pair_logits(p + 1)
            apply_pending(ct_prev, slice(2 * p * Q_BLK, (2 * p + 2) * Q_BLK))
            for hh in range(2):
                h = 2 * p + hh
                ls = slice(h * Q_BLK, (h + 1) * Q_BLK)
                lg = lg2[:, hh * Q_BLK:(hh + 1) * Q_BLK] + addm
                if near:
                    lg = lg + bt_ref[i - 2 * j, h]
                m_prev = m_scr[:, ls]
                m_new = jnp.maximum(m_prev, jnp.max(lg, axis=0, keepdims=True))
                m_scr[:, ls] = m_new
                alpha_scr[:, ls] = jnp.exp2(m_prev - m_new)
                p_scr[:, ls] = jnp.exp2(lg - m_new).astype(BF16)
        return carry

    n_far = jnp.maximum(i - N_BIAS_TILES + 2, 0) // 2
    lax.fori_loop(0, n_far, functools.partial(attn_tile, near=False), 0)
    lax.fori_loop(n_far, n_tiles, functools.partial(attn_tile, near=True), 0)
    ct_last = ct_ref[:, pl.ds(pl.multiple_of((n_tiles - 1) * K_TILE, K_TILE), K_TILE)]
    for p in range(H // 2):
        apply_pending(ct_last, slice(2 * p * Q_BLK, (2 * p + 2) * Q_BLK))

    o_lat = (acc_scr[:KV_RANK, :] / acc_scr[KV_RANK:KV_RANK + 1, :]).T.astype(BF16)
    for p in range(H // 2):
        pair = jnp.concatenate([o_lat[2 * p * Q_BLK:(2 * p + 1) * Q_BLK], o_lat[(2 * p + 1) * Q_BLK:(2 * p + 2) * Q_BLK]],
                               axis=1)
        o_out[:, p * LANES:(p + 1) * LANES] = _mm(pair, wuv_ref[p])


ONES_ROWS = 16


def _dsa_operands(pp, c):
    ik = pp[:, 896:960]
    ik_hi = ik.astype(BF16)
    ik_lo = (ik - ik_hi.astype(F32)).astype(BF16)
    ik_packed = jnp.concatenate([ik_hi, ik_lo, ik_hi, jnp.zeros_like(ik_hi)], axis=1)
    iwt = jnp.transpose(pp[:, 960:968])
    ct_ext = jnp.concatenate([jnp.transpose(c), jnp.ones((ONES_ROWS, c.shape[0]), c.dtype)], axis=0)
    return iwt, ik_packed, c, ct_ext


def _dsa(pp, iwt, ik_packed, c, ct_ext, ql, bt, wuv2, *, topk):
    T = pp.shape[0]
    H = SA_HEADS
    assert T % (R_TILES * K_TILE) == 0 and N_BIAS_TILES * Q_BLK - (K_TILE - 1) >= _BUCKET_BOUNDS[-1]
    whole = pl.BlockSpec(memory_space=pltpu.VMEM)
    return pl.pallas_call(
        functools.partial(_dsa_kernel, topk=topk),
        grid=(T // Q_BLK,),
        in_specs=[pl.BlockSpec((Q_BLK, 512), lambda i: (i, 0)),
                  pl.BlockSpec((IDX_HEADS, Q_BLK), lambda i: (0, i)),
                  pl.BlockSpec((H, Q_BLK, KV_RANK), lambda i: (0, i, 0)),
                  whole, whole, whole, whole, whole],
        out_specs=pl.BlockSpec((Q_BLK, 1024), lambda i: (i, 0)),
        out_shape=jax.ShapeDtypeStruct((T, 1024), F32),
        scratch_shapes=[pltpu.VMEM((T, Q_BLK), I32),
                        pltpu.VMEM((IDX_HEADS // 2, 2 * Q_BLK, 2 * LANES), BF16),
                        pltpu.VMEM((1, H * Q_BLK), F32),
                        pltpu.VMEM((1, H * Q_BLK), F32),
                        pltpu.VMEM((K_TILE, H * Q_BLK), BF16),
                        pltpu.VMEM((KV_RANK + ONES_ROWS, H * Q_BLK), F32)],
        compiler_params=_cp(("parallel",), 56),
        name="dsa",
    )(pp, iwt, ql, ik_packed, c, ct_ext, bt, wuv2)


def _merge_kernel(y_ref, bonus_ref, g_ref, o_ref, ga_ref, gb_ref, x_ref, lnw_ref, lnb_ref, bd_ref,
                  wa_ref, wb_ref, wout_ref, h_out):
    bd = bd_ref[...]
    wide = bd.shape[0]

    def head_mean(x):
        hi, lo = _split2(x)
        parts = [_mm(hi[:, p * wide:(p + 1) * wide], bd) + _mm(lo[:, p * wide:(p + 1) * wide], bd)
                 for p in range(RW_WIDTH // wide)]
        return jnp.concatenate(parts, axis=1) * (1.0 / 64)

    y = y_ref[...]
    yc = y - head_mean(y)
    yn = yc * lax.rsqrt(head_mean(yc * yc) + RW_GN_EPS) * lnw_ref[...] + lnb_ref[...]
    ya = (yn + bonus_ref[...]) * g_ref[...]
    pa = _mm(ya.astype(BF16), wa_ref[...])
    pb = _mm(o_ref[...].astype(BF16), wb_ref[...])
    m = jax.nn.sigmoid(ga_ref[...]) * pa + jax.nn.sigmoid(gb_ref[...]) * pb
    h_out[...] = x_ref[...] + _mm(m.astype(BF16), wout_ref[...])


def _merge(y, bonus, g, o, pm, x, ln_w, ln_b, bd, wa, wb, wout, *, tm):
    T, D = x.shape
    W = RW_WIDTH
    row = lambda a: a.reshape(1, -1)
    act = pl.BlockSpec((tm, W), lambda i: (i, 0))
    full = lambda shape: pl.BlockSpec(shape, lambda i: (0,) * len(shape))
    whole = pl.BlockSpec(memory_space=pltpu.VMEM)
    return pl.pallas_call(
        _merge_kernel,
        grid=(T // tm,),
        in_specs=[act, act, act, act,
                  pl.BlockSpec((tm, D), lambda i: (i, 0)),
                  pl.BlockSpec((tm, D), lambda i: (i, 1)),
                  pl.BlockSpec((tm, D), lambda i: (i, 0)),
                  full((1, W)), full((1, W)), whole, whole, whole, whole],
        out_specs=pl.BlockSpec((tm, D), lambda i: (i, 0)),
        out_shape=jax.ShapeDtypeStruct((T, D), F32),
        compiler_params=_cp(("parallel",), 56),
        name="merge",
    )(y, bonus, g, o, pm, pm, x, row(ln_w), row(ln_b), bd[:2 * LANES, :2 * LANES], wa, wb, wout)


def _sort16_pairs():
    n, pairs, p = 16, [], 1
    while p < n:
        k = p
        while k >= 1:
            for j in range(k % p, n - k, 2 * k):
                for i in range(min(k, n - j - k)):
                    if (i + j) // (2 * p) == (i + j + k) // (2 * p):
                        pairs.append((i + j, i + j + k))
            k //= 2
        p *= 2
    return pairs


_SORT16 = _sort16_pairs()
PEER_MARGIN = 1.0 - 2.0 ** -20


def _exchange(a, i, j):
    a[i], a[j] = jnp.maximum(a[i], a[j]), jnp.minimum(a[i], a[j])


def _bitonic_sort16(a):
    for dist in (8, 4, 2, 1):
        for i in range(PEER_TOPK):
            if not i & dist:
                _exchange(a, i, i + dist)
    return a


def _top16_merge(x, y):
    return _bitonic_sort16([jnp.maximum(x[i], y[PEER_TOPK - 1 - i]) for i in range(PEER_TOPK)])


def _peer_select_kernel(qp_ref, keys_ref, et_out, tau_out):
    tm = qp_ref.shape[1]
    sub = lax.broadcasted_iota(I32, (SUBLANES, tm), 0)
    zero = jnp.zeros((SUBLANES, tm), F32)

    def per_head(h, packed):
        packed = list(packed)
        for c in range(2):
            hc = 2 * h + c
            s_t = _dot3(keys_ref[hc], qp_ref[hc], _mm_nt)
            e = jnp.exp(s_t - jnp.max(s_t, axis=0, keepdims=True))
            et_out[hc] = e
            a = [e[g * SUBLANES:(g + 1) * SUBLANES, :] for g in range(PEER_KEYS // SUBLANES)]
            for i, j in _SORT16:
                _exchange(a, i, j)
            for shift in (4, 2, 1):
                a = _top16_merge(a, [pltpu.roll(t, shift, 0) for t in a])
            for i in range(PEER_TOPK):
                packed[c * PEER_TOPK + i] = jnp.where(sub == h, a[i], packed[c * PEER_TOPK + i])
        return tuple(packed)

    packed = lax.fori_loop(0, PEER_HEADS, per_head, (zero,) * (2 * PEER_TOPK))
    top_a, top_b = packed[:PEER_TOPK], packed[PEER_TOPK:]

    def run(i, n):
        return [top_a[i] * top_b[j] for j in range(n)]

    g1 = run(0, 16)
    g2 = _bitonic_sort16(run(1, 8) + [top_a[i] * top_b[0] for i in range(PEER_TOPK - 1, 7, -1)])
    g3 = run(2, 5) + run(3, 4) + run(4, 3) + run(5, 2) + run(6, 2)
    for i, j in _SORT16:
        _exchange(g3, i, j)
    top = _top16_merge(_top16_merge(_top16_merge(g1, g2), g3), run(7, 2) + [zero] * 14)
    z = top[0]
    for t in top[1:]:
        z = z + t
    zinv = 1.0 / z
    tau_out[...] = top[PEER_TOPK - 1] * PEER_MARGIN * zinv
    for h in range(PEER_HEADS):
        et_out[2 * h] = et_out[2 * h] * zinv[h:h + 1, :]


def _peer_select(qp, keys, *, tm):
    T = qp.shape[1]
    return pl.pallas_call(
        _peer_select_kernel,
        grid=(T // tm,),
        in_specs=[pl.BlockSpec((2 * PEER_HEADS, tm, LANES), lambda i: (0, i, 0)),
                  pl.BlockSpec((2 * PEER_HEADS, PEER_KEYS, LANES), lambda i: (0, 0, 0))],
        out_specs=[pl.BlockSpec((2 * PEER_HEADS, PEER_KEYS, tm), lambda i: (0, 0, i)),
                   pl.BlockSpec((SUBLANES, tm), lambda i: (0, i))],
        out_shape=[jax.ShapeDtypeStruct((2 * PEER_HEADS, PEER_KEYS, T), F32),
                   jax.ShapeDtypeStruct((PEER_HEADS, T), F32)],
        compiler_params=_cp(("parallel",)),
        name="peer_select",
    )(qp, keys)


def _peer_kernel(h_ref, gf_ref, gout_ref, u_ref, vt_ref, et_ref, tau_ref, out_ref, hn_scr, acc_scr, ga0_scr, ga1_scr,
                 *, e_tile):
    j = pl.program_id(1)
    n_tiles = pl.num_programs(1) - 1
    tm = h_ref.shape[0]
    per_tile = e_tile // PEER_KEYS

    @pl.when(j == 0)
    def _():
        h = h_ref[...]
        hn = h * lax.rsqrt(jnp.mean(h * h, axis=-1, keepdims=True) + EPS) * gf_ref[...]
        hn_scr[...] = hn.astype(BF16)
        acc_scr[...] = jnp.zeros_like(acc_scr)
        ga1_scr[...] = jnp.zeros_like(ga1_scr)

    d_model = vt_ref.shape[0]
    half = e_tile // 2

    def pre_act(s):
        return _mm_nt(u_ref[s * half:(s + 1) * half, :], hn_scr[...])

    def apply_values(q, ga_read):
        rows = slice(q * (d_model // 4), (q + 1) * (d_model // 4))
        acc_scr[rows, :] += _mm(vt_ref[rows, :], ga_read[...])

    RC = 32

    def gated(s, pre, ga_write):
        for ai in range(per_tile // 2):
            a = jnp.minimum(j, n_tiles - 1) * per_tile + s * (per_tile // 2) + ai
            e1 = [et_ref[2 * h, pl.ds(a, 1), :] for h in range(PEER_HEADS)]
            for rc in range(PEER_KEYS // RC):
                x = pre[ai * PEER_KEYS + rc * RC:ai * PEER_KEYS + (rc + 1) * RC, :]
                act = 0.5 * x * (1.0 + lax.erf(x * (2.0 ** -0.5)))
                gate = jnp.zeros((RC, tm), F32)
                for h in range(PEER_HEADS):
                    w = et_ref[2 * h + 1, rc * RC:(rc + 1) * RC, :] * e1[h]
                    gate = gate + jnp.where(w >= tau_ref[h:h + 1, :], w, 0.0)
                r0 = s * half + ai * PEER_KEYS + rc * RC
                ga_write[r0:r0 + RC, :] = (gate * act).astype(BF16)

    def step(ga_write, ga_read):
        pre0 = pre_act(0)
        apply_values(0, ga_read)
        pre1 = pre_act(1)
        gated(0, pre0, ga_write)
        apply_values(1, ga_read)
        apply_values(2, ga_read)
        apply_values(3, ga_read)
        gated(1, pre1, ga_write)

    @pl.when(j % 2 == 0)
    def _():
        step(ga0_scr, ga1_scr)

    @pl.when(j % 2 == 1)
    def _():
        step(ga1_scr, ga0_scr)

    @pl.when(j == n_tiles)
    def _():
        h3 = h_ref[...] + acc_scr[...].T
        out_ref[...] = h3 * lax.rsqrt(jnp.mean(h3 * h3, axis=-1, keepdims=True) + EPS) * gout_ref[...]


def _peer(h1, g_ffn, g_out, u, vt, et, tau, *, tm, e_tile):
    T, D = h1.shape
    E = u.shape[0]
    n_tiles = E // e_tile
    assert n_tiles % 2 == 0
    return pl.pallas_call(
        functools.partial(_peer_kernel, e_tile=e_tile),
        grid=(T // tm, n_tiles + 1),
        in_specs=[pl.BlockSpec((tm, D), lambda i, j: (i, 0)),
                  pl.BlockSpec((1, D), lambda i, j: (0, 0)),
                  pl.BlockSpec((1, D), lambda i, j: (0, 0)),
                  pl.BlockSpec((e_tile, D), lambda i, j: (jnp.minimum(j, n_tiles - 1), 0)),
                  pl.BlockSpec((D, e_tile), lambda i, j: (0, jnp.maximum(j - 1, 0))),
                  pl.BlockSpec((2 * PEER_HEADS, PEER_KEYS, tm), lambda i, j: (0, 0, i)),
                  pl.BlockSpec((PEER_HEADS, tm), lambda i, j: (0, i))],
        out_specs=pl.BlockSpec((tm, D), lambda i, j: (i, 0)),
        out_shape=jax.ShapeDtypeStruct((T, D), F32),
        scratch_shapes=[pltpu.VMEM((tm, D), BF16), pltpu.VMEM((D, tm), F32),
                        pltpu.VMEM((e_tile, tm), BF16), pltpu.VMEM((e_tile, tm), BF16)],
        compiler_params=_cp(("parallel", "arbitrary"), 56),
        name="peer",
    )(h1, g_ffn.reshape(1, D), g_out.reshape(1, D), u, vt, et, tau)


def _pad_cols(a, n):
    return jnp.pad(a, ((0, 0), (0, n - a.shape[1])))


def _block_diag2(a, b):
    za = jnp.zeros((a.shape[0], b.shape[1]), a.dtype)
    zb = jnp.zeros((b.shape[0], a.shape[1]), a.dtype)
    return jnp.concatenate([jnp.concatenate([a, za], axis=1), jnp.concatenate([zb, b], axis=1)], axis=0)


def kernel(x, norm_mix_g, w_in, shift_mix, w0, w_decay_up, a0, w_icl_up, w_gate_up, k_k, k_a, r_k, ln_x_w, ln_x_b,
           kv_norm_g, w_uk, w_uv, rel_bias, w_branch_rwkv, w_branch_dsa, w_out, norm_ffn_g, w_peer_query,
           peer_sub_keys, peer_u, peer_v, norm_final_g):
    B, T, D = x.shape
    assert B == 1 and D == D_MODEL and T % (2 * K_TILE) == 0 and norm_mix_g.shape[0] == 1
    topk = min(TOPK_MAX, T // 4)
    x2 = x[0]
    w = w_in[0]
    sm = shift_mix[0]
    o_zw, o_za, o_zg, o_q, o_ckv, o_iq, o_ik, o_iw, o_g = 3072, 3136, 3200, 3360, 4384, 4640, 5152, 5216, 5224
    w_main = jnp.concatenate([w[:, o_g:o_g + 4096], w[:, :3072], w[:, o_q:o_q + 1024], w[:, o_ckv:o_ckv + 256]],
                             axis=1).astype(BF16)
    w_prec = jnp.concatenate([w[:, o_iq:o_iq + 512], _pad_cols(w[:, o_zg:o_zg + 160], 256), w[:, o_zw:o_zw + 128],
                              _pad_cols(w[:, o_ik:o_ik + 72], 128)], axis=1)
    tm = min(512, T)
    pm = _proj(x2, norm_mix_g[0], w_main, precise=False, tm=tm, tn=768, name="proj_main")
    pp = _proj(x2, norm_mix_g[0], w_prec, precise=True, tm=tm, tn=512, name="proj_precise")

    head_id = jnp.arange(RW_WIDTH) // 64
    bd = (head_id[:, None] == head_id[None, :]).astype(BF16)
    mixes = [sm[:1024], sm[1024:2048], sm[2048:3072], sm[o_zw:o_zw + 128],
             jnp.pad(sm[o_zg:o_zg + 160], (0, 96))]
    wd = jnp.pad(w_decay_up[0], ((0, 64), (0, 0)))
    wi = jnp.pad(w_icl_up[0], ((64, 0), (0, 0)))
    wg = jnp.pad(w_gate_up[0], ((0, 96), (0, 0)))
    r, k4, v, kkn, b, lw, g, bonus = _rw_prep(pm, pp, mixes, w0[0], wd, a0[0], wi, wg, k_k[0], k_a[0],
                                              r_k[0].reshape(-1), bd, tm=min(256, T))
    nch = T // RW_CHUNK
    p_mat, q_mat, rp, y0 = _rw_chunk(r, k4, v, kkn, b, lw, G=min(8, nch))
    y = _rw_scan(p_mat, q_mat, rp, y0)

    wuk_t = jnp.transpose(w_uk[0], (0, 2, 1))
    wuk2 = jnp.stack([_block_diag2(wuk_t[2 * p], wuk_t[2 * p + 1]) for p in range(SA_HEADS // 2)]).astype(BF16)
    wuv2 = jnp.stack([_block_diag2(w_uv[0][2 * p], w_uv[0][2 * p + 1]) for p in range(SA_HEADS // 2)]).astype(BF16)
    c, ql = _dsa_prep(pm, kv_norm_g[0], wuk2, tm=min(512, T))
    o_dsa = _dsa(pp, *_dsa_operands(pp, c), ql, _rel_bias_tiles(rel_bias), wuv2, topk=topk)

    h1 = _merge(y, bonus, g, o_dsa, pm, x2, ln_x_w[0], ln_x_b[0], bd,
                w_branch_rwkv[0].astype(BF16), w_branch_dsa[0].astype(BF16), w_out[0].astype(BF16),
                tm=min(256, T))

    qp = _proj(h1, norm_ffn_g[0], w_peer_query[0], precise=True, tm=tm, tn=512, name="peer_query", lane_groups=True)
    keys = peer_sub_keys[0].reshape(2 * PEER_HEADS, PEER_KEYS, LANES)
    et, tau = _peer_select(qp, keys, tm=128)
    u = peer_u[0].astype(BF16)
    vt = jnp.transpose(peer_v[0]).astype(BF16)
    out = _peer(h1, norm_ffn_g[0], norm_final_g, u, vt, et, tau, tm=min(512, T), e_tile=512)
    return out[None]
```

```python
import functools
import math

import numpy as np
import jax
import jax.numpy as jnp
from jax import lax
from jax.experimental import pallas as pl
from jax.experimental.pallas import tpu as pltpu

F32, BF16, I32 = jnp.float32, jnp.bfloat16, jnp.int32

D_MODEL = 2048
EPS = 1e-6
RW_WIDTH = 1024
RW_GN_EPS = 64e-5
RW_CHUNK = 128
RW_LOCKSTEP = 8
SA_HEADS = 16
SA_SCALE = 64 ** -0.5
KV_RANK = 256
IDX_HEADS = 8
IDX_SCALE = (8 * 64) ** -0.5
TOPK_MAX = 256
REL_BUCKETS = 32
REL_MAX_DIST = 1024
PEER_HEADS = 8
PEER_KEYS = 128
PEER_TOPK = 16
LANES = 128
SUBLANES = 8
NEG_BIG = -1e30
INT_MIN = -2 ** 31

Q_BLK = 128
K_TILE = 256
N_BIAS_TILES = 7
R_TILES = 4
LOG2E = math.log2(math.e)


def _cp(sem, vmem_mb=48):
    return pltpu.CompilerParams(dimension_semantics=sem, vmem_limit_bytes=vmem_mb * 1024 * 1024)


def _bf(x):
    return x.astype(BF16)


def _split2(x):
    hi = x.astype(BF16)
    lo = (x - hi.astype(F32)).astype(BF16)
    return hi, lo


def _mm(a, b):
    return jnp.dot(a, b, preferred_element_type=F32)


def _mm_nt(a, b):
    return lax.dot_general(a, b, (((1,), (1,)), ((), ())), preferred_element_type=F32)


def _mm_tn(a, b):
    return lax.dot_general(a, b, (((0,), (0,)), ((), ())), preferred_element_type=F32)


def _d3(a_s, b_s, mm=_mm):
    return mm(a_s[0], b_s[0]) + (mm(a_s[0], b_s[1]) + mm(a_s[1], b_s[0]))


def _dot3(a, b, mm=_mm):
    return _d3(_split2(a), _split2(b), mm)


def _proj_kernel(x_ref, g_ref, w_ref, o_ref, xh_ref, xl_ref, *, precise, lane_groups):
    @pl.when(pl.program_id(1) == 0)
    def _():
        x = x_ref[...]
        xn = x * lax.rsqrt(jnp.mean(x * x, axis=-1, keepdims=True) + EPS) * g_ref[...]
        hi = xn.astype(BF16)
        xh_ref[...] = hi
        if precise:
            xl_ref[...] = (xn - hi.astype(F32)).astype(BF16)

    if precise:
        wh, wl = _split2(w_ref[...])
        xh = xh_ref[...]
        res = _mm(xh, wh) + (_mm(xh, wl) + _mm(xl_ref[...], wh))
    else:
        res = _mm(xh_ref[...], w_ref[...])
    if lane_groups:
        for k in range(res.shape[1] // LANES):
            o_ref[k] = res[:, k * LANES:(k + 1) * LANES]
    else:
        o_ref[...] = res


def _proj(x, g, w, *, precise, tm, tn, name, lane_groups=False):
    T, D = x.shape
    N = w.shape[1]
    lo_rows = tm if precise else SUBLANES * 2
    if lane_groups:
        out_spec = pl.BlockSpec((tn // LANES, tm, LANES), lambda i, j: (j, i, 0))
        out_shape = jax.ShapeDtypeStruct((N // LANES, T, LANES), F32)
    else:
        out_spec = pl.BlockSpec((tm, tn), lambda i, j: (i, j))
        out_shape = jax.ShapeDtypeStruct((T, N), F32)
    return pl.pallas_call(
        functools.partial(_proj_kernel, precise=precise, lane_groups=lane_groups),
        grid=(T // tm, N // tn),
        in_specs=[pl.BlockSpec((tm, D), lambda i, j: (i, 0)),
                  pl.BlockSpec((1, D), lambda i, j: (0, 0)),
                  pl.BlockSpec((D, tn), lambda i, j: (0, j))],
        out_specs=out_spec,
        out_shape=out_shape,
        scratch_shapes=[pltpu.VMEM((tm, D), BF16), pltpu.VMEM((lo_rows, D), BF16)],
        compiler_params=_cp(("parallel", "arbitrary")),
        name=name,
    )(x, g.reshape(1, D), w)


def _rw_prep_kernel(r_ref, k_ref, v_ref, wa_ref, zg_ref, rp_ref, kp_ref, vp_ref, wap_ref, zgp_ref,
                    mr_ref, mk_ref, mv_ref, mwa_ref, mg_ref, w0_ref, wd_ref, a0_ref, wi_ref, wg_ref,
                    kk_ref, ka_ref, rk_ref, bd_ref,
                    r_out, k_out, v_out, kkn_out, b_out, lw_out, g_out, bonus_out):
    first = pl.program_id(0) == 0

    def shift(p_ref, prev_ref, mix_ref):
        p = p_ref[...]
        last = jnp.where(first, 0.0, prev_ref[SUBLANES - 1:SUBLANES, :])
        row = lax.broadcasted_iota(I32, p.shape, 0)
        prev = jnp.where(row == 0, last, pltpu.roll(p, 1, 0))
        return p + (prev - p) * mix_ref[...]

    r = shift(r_ref, rp_ref, mr_ref)
    k = shift(k_ref, kp_ref, mk_ref)
    v = shift(v_ref, vp_ref, mv_ref)
    zwa = shift(wa_ref, wap_ref, mwa_ref)
    zg = shift(zg_ref, zgp_ref, mg_ref)

    nd = -(w0_ref[...] + _dot3(jnp.tanh(zwa), wd_ref[...]))
    softplus = jnp.maximum(nd, 0.0) + jnp.log1p(jnp.exp(-jnp.abs(nd)))
    lw = -jnp.exp(-softplus - 0.5)
    a = jax.nn.sigmoid(a0_ref[...] + _dot3(zwa, wi_ref[...]))
    g = _dot3(jax.nn.sigmoid(zg), wg_ref[...])

    bd = bd_ref[...]
    wide = bd.shape[0]

    def head_sum(x):
        hi, lo = _split2(x)
        parts = [_mm(hi[:, p * wide:(p + 1) * wide], bd) + _mm(lo[:, p * wide:(p + 1) * wide], bd)
                 for p in range(RW_WIDTH // wide)]
        return jnp.concatenate(parts, axis=1)

    kk = k * kk_ref[...]
    kkn = kk / jnp.maximum(jnp.sqrt(head_sum(kk * kk)), 1e-12)
    k4 = k * (1.0 + (a - 1.0) * ka_ref[...])
    r_out[...] = r
    k_out[...] = k4
    v_out[...] = v
    kkn_out[...] = kkn
    b_out[...] = kkn * a
    lw_out[...] = lw
    g_out[...] = g
    bonus_out[...] = head_sum(r * k4 * rk_ref[...]) * v


def _rw_prep(pm, pp, mixes, w0, wd, a0, wi, wg, k_k, k_a, r_k, bd, *, tm):
    T = pm.shape[0]
    W = RW_WIDTH
    pb = tm // SUBLANES

    def cur(width, col):
        return pl.BlockSpec((tm, width), lambda i: (i, col))

    def prev(width, col):
        return pl.BlockSpec((SUBLANES, width), lambda i: (jnp.maximum(i * pb - 1, 0), col))

    def full(shape):
        return pl.BlockSpec(shape, lambda i: (0,) * len(shape))

    row = lambda a: a.reshape(1, -1)
    out_sd = jax.ShapeDtypeStruct((T, W), F32)
    return pl.pallas_call(
        _rw_prep_kernel,
        grid=(T // tm,),
        in_specs=[cur(W, 4), cur(W, 5), cur(W, 6), cur(128, 6), cur(256, 2),
                  prev(W, 4), prev(W, 5), prev(W, 6), prev(128, 6), prev(256, 2),
                  full((1, W)), full((1, W)), full((1, W)), full((1, 128)), full((1, 256)),
                  full((1, W)), full((128, W)), full((1, W)), full((128, W)), full((256, W)),
                  full((1, W)), full((1, W)), full((1, W)), full((2 * LANES, 2 * LANES))],
        out_specs=[pl.BlockSpec((tm, W), lambda i: (i, 0))] * 8,
        out_shape=[out_sd] * 8,
        compiler_params=_cp(("parallel",)),
        name="rw_prep",
    )(pm, pm, pm, pp, pp, pm, pm, pm, pp, pp,
      *[row(m) for m in mixes], row(w0), wd, row(a0), wi, wg, row(k_k), row(k_a), row(r_k), bd[:2 * LANES, :2 * LANES])


def _rw_chunk_kernel(r_ref, k_ref, v_ref, kkn_ref, b_ref, lw_ref, p_out, q_out, rp_out, y0_out, *, G):
    C = RW_CHUNK
    row = lax.broadcasted_iota(I32, (C, C), 0)
    col = lax.broadcasted_iota(I32, (C, C), 1)
    incl = row >= col
    strict = row > col
    eye = row == col
    same_head = (row // 64) == (col // 64)
    tri = jnp.where(incl, 1.0, 0.0).astype(BF16)
    lane = lax.broadcasted_iota(I32, (C, LANES), 1)
    eye_f = jnp.where(eye, 1.0, 0.0)

    head_lanes = (lane < 64, lane >= 64)
    U = RW_LOCKSTEP

    def body(cu, carry):
        ch = [cu * U + u for u in range(U)]
        sls = [pl.ds(pl.multiple_of(c * C, C), C) for c in ch]
        pre = []
        for sl in sls:
            r, k4, v, kkn, b, lw = (ref[sl, :] for ref in (r_ref, k_ref, v_ref, kkn_ref, b_ref, lw_ref))
            h1 = lw.astype(BF16)
            r1 = lw - h1.astype(F32)
            h2 = r1.astype(BF16)
            h3 = (r1 - h2.astype(F32)).astype(BF16)
            L = _mm(tri, h1) + (_mm(tri, h2) + _mm(tri, h3))
            Ltot = L[C - 1:C, :]
            Lr = L - L[C // 2 - 1:C // 2, :]
            eneg = jnp.exp(-Lr)
            tail = jnp.exp(Ltot - L)
            pre.append(dict(
                rhat=r * jnp.exp(Lr), ahat=-kkn * jnp.exp(Lr - lw), kt=_bf(k4 * eneg), bt=_bf(b * eneg),
                rabs=r * jnp.exp(L), aabs=-kkn * jnp.exp(L - lw), kbar=_bf(k4 * tail), bbar=_bf(b * tail),
                gam=jnp.exp(Ltot), vb=_bf(v)))
        inst = [(u, hh) for u in range(U) for hh in range(2)]
        am = [_bf(jnp.where(head_lanes[hh], pre[u]["ahat"], 0.0)) for u, hh in inst]
        rm = [_bf(jnp.where(head_lanes[hh], pre[u]["rhat"], 0.0)) for u, hh in inst]
        a_ab = [jnp.where(strict, _mm_nt(am[k], pre[u]["bt"]), 0.0) for k, (u, hh) in enumerate(inst)]
        a_ak = [_bf(jnp.where(strict, _mm_nt(am[k], pre[u]["kt"]), 0.0)) for k, (u, hh) in enumerate(inst)]
        m_rb = [_bf(jnp.where(incl, _mm_nt(rm[k], pre[u]["bt"]), 0.0)) for k, (u, hh) in enumerate(inst)]
        m_rk = [_bf(jnp.where(incl, _mm_nt(rm[k], pre[u]["kt"]), 0.0)) for k, (u, hh) in enumerate(inst)]
        n = a_ab
        x = [eye_f + t for t in n]
        for _ in range(6):
            nb = [_bf(t) for t in n]
            n = [_mm(t, t) for t in nb]
            x = [xi + _mm(_bf(xi), _bf(ni)) for xi, ni in zip(x, n)]
        xb = [_bf(t) for t in x]
        akv = [_bf(_mm(a_ak[k], pre[u]["vb"])) for k, (u, hh) in enumerate(inst)]
        ah_h = [_mm(xb[k], _bf(jnp.where(head_lanes[hh], pre[u]["aabs"], 0.0))) for k, (u, hh) in enumerate(inst)]
        u0_h = [_mm(xb[k], akv[k]) for k in range(len(inst))]
        y0_h = [_mm(m_rk[k], pre[u]["vb"]) + _mm(m_rb[k], _bf(u0_h[k])) for k, (u, hh) in enumerate(inst)]
        rp_h = [jnp.where(head_lanes[hh], pre[u]["rabs"], 0.0) + _mm(m_rb[k], _bf(ah_h[k]))
                for k, (u, hh) in enumerate(inst)]
        for u in range(U):
            lo, hi = 2 * u, 2 * u + 1
            ah = _bf(ah_h[lo] + ah_h[hi])
            u0 = _bf(jnp.where(head_lanes[0], u0_h[lo], u0_h[hi]))
            pu = pre[u]
            p_out[0, ch[u]] = jnp.where(eye, pu["gam"], 0.0) + jnp.where(same_head, _mm_tn(pu["bbar"], ah), 0.0)
            q_out[0, ch[u]] = jnp.where(same_head, _mm_tn(pu["kbar"], pu["vb"]) + _mm_tn(pu["bbar"], u0), 0.0)
            rp_out[sls[u], :] = rp_h[lo] + rp_h[hi]
            y0_out[sls[u], :] = jnp.where(head_lanes[0], y0_h[lo], y0_h[hi])
        return carry

    lax.fori_loop(0, G // U, body, 0)


def _rw_chunk(r, k4, v, kkn, b, lw, *, G):
    T = r.shape[0]
    C = RW_CHUNK
    nch = T // C
    rows = G * C
    cur = pl.BlockSpec((rows, LANES), lambda p, g: (g, p))
    mat = pl.BlockSpec((1, G, C, C), lambda p, g: (p, g, 0, 0))
    return pl.pallas_call(
        functools.partial(_rw_chunk_kernel, G=G),
        grid=(RW_WIDTH // LANES, nch // G),
        in_specs=[cur] * 6,
        out_specs=[mat, mat, cur, cur],
        out_shape=[jax.ShapeDtypeStruct((RW_WIDTH // LANES, nch, C, C), F32)] * 2
        + [jax.ShapeDtypeStruct((T, RW_WIDTH), F32)] * 2,
        compiler_params=_cp(("parallel", "parallel")),
        name="rw_chunk",
    )(r, k4, v, kkn, b, lw)


def _rw_scan_kernel(p_ref, q_ref, rp_ref, y0_ref, y_out, s_scr):
    @pl.when(pl.program_id(0) == 0)
    def _():
        s_scr[...] = jnp.zeros_like(s_scr)

    for p in range(RW_WIDTH // LANES):
        ls = slice(p * LANES, (p + 1) * LANES)
        s0 = _bf(s_scr[p])
        y_out[:, ls] = _mm(_bf(rp_ref[:, ls]), s0) + y0_ref[:, ls]
        s_scr[p] = _mm(_bf(p_ref[p, 0]), s0) + q_ref[p, 0]


def _rw_scan(p_mat, q_mat, rp, y0):
    T = rp.shape[0]
    C = RW_CHUNK
    npair = RW_WIDTH // LANES
    mat = pl.BlockSpec((npair, 1, C, C), lambda c: (0, c, 0, 0))
    cur = pl.BlockSpec((C, RW_WIDTH), lambda c: (c, 0))
    return pl.pallas_call(
        _rw_scan_kernel,
        grid=(T // C,),
        in_specs=[mat, mat, cur, cur],
        out_specs=cur,
        out_shape=jax.ShapeDtypeStruct((T, RW_WIDTH), F32),
        scratch_shapes=[pltpu.VMEM((npair, C, C), F32)],
        compiler_params=_cp(("arbitrary",)),
        name="rw_scan",
    )(p_mat, q_mat, rp, y0)


def _dsa_prep_kernel(q_ref, ckv_ref, g_ref, wuk_ref, c_out, ql_out):
    ckv = ckv_ref[...]
    c = ckv * lax.rsqrt(jnp.mean(ckv * ckv, axis=-1, keepdims=True) + EPS) * g_ref[...]
    c_out[...] = c.astype(BF16)
    for p in range(SA_HEADS // 2):
        qp = q_ref[:, p * LANES:(p + 1) * LANES].astype(BF16)
        ql = _mm(qp, wuk_ref[p]) * (SA_SCALE * LOG2E)
        ql_out[2 * p] = ql[:, :KV_RANK].astype(BF16)
        ql_out[2 * p + 1] = ql[:, KV_RANK:].astype(BF16)


def _dsa_prep(pm, kv_norm_g, wuk2, *, tm):
    T = pm.shape[0]
    return pl.pallas_call(
        _dsa_prep_kernel,
        grid=(T // tm,),
        in_specs=[pl.BlockSpec((tm, 1024), lambda i: (i, 7)),
                  pl.BlockSpec((tm, KV_RANK), lambda i: (i, 32)),
                  pl.BlockSpec((1, KV_RANK), lambda i: (0, 0)),
                  pl.BlockSpec((SA_HEADS // 2, LANES, 2 * KV_RANK), lambda i: (0, 0, 0))],
        out_specs=[pl.BlockSpec((tm, KV_RANK), lambda i: (i, 0)),
                   pl.BlockSpec((SA_HEADS, tm, KV_RANK), lambda i: (0, i, 0))],
        out_shape=[jax.ShapeDtypeStruct((T, KV_RANK), BF16),
                   jax.ShapeDtypeStruct((SA_HEADS, T, KV_RANK), BF16)],
        compiler_params=_cp(("parallel",)),
        name="dsa_prep",
    )(pm, pm, kv_norm_g.reshape(1, KV_RANK), wuk2)


def _bucket_bounds():
    nb = REL_BUCKETS // 2
    max_exact = nb // 2
    n = np.arange(1, 2 * REL_MAX_DIST, dtype=np.int64)
    large = max_exact + (np.log(n.astype(np.float32) / max_exact)
                         / math.log(REL_MAX_DIST / max_exact) * (nb - max_exact)).astype(np.int32)
    large = np.minimum(large, nb - 1)
    bounds = [int(n[np.argmax(large >= b)]) for b in range(max_exact + 1, nb)]
    assert all(b2 > b1 for b1, b2 in zip(bounds, bounds[1:]))
    return bounds


_BUCKET_BOUNDS = _bucket_bounds()


def _rel_bias_kernel(rb_ref, bt_out):
    o = pl.program_id(0)
    sl = lax.broadcasted_iota(I32, (K_TILE, Q_BLK), 0)
    tl = lax.broadcasted_iota(I32, (K_TILE, Q_BLK), 1)
    rel = sl - tl - o * Q_BLK
    nb = REL_BUCKETS // 2
    max_exact = nb // 2
    n = jnp.abs(rel)
    large = jnp.full(rel.shape, max_exact, I32)
    for bound in _BUCKET_BOUNDS:
        large = large + jnp.where(n >= bound, 1, 0)
    bucket = jnp.where(rel > 0, nb, 0) + jnp.where(n < max_exact, n, large)
    for h in range(SA_HEADS):
        acc = jnp.zeros(rel.shape, F32)
        for bkt in range(REL_BUCKETS):
            acc = jnp.where(bucket == bkt, rb_ref[bkt, h], acc)
        bt_out[0, h] = (acc - rb_ref[nb - 1, h]) * LOG2E


def _rel_bias_tiles(rel_bias):
    return pl.pallas_call(
        _rel_bias_kernel,
        grid=(N_BIAS_TILES,),
        in_specs=[pl.BlockSpec(memory_space=pltpu.SMEM)],
        out_specs=pl.BlockSpec((1, SA_HEADS, K_TILE, Q_BLK), lambda o: (o, 0, 0, 0)),
        out_shape=jax.ShapeDtypeStruct((N_BIAS_TILES, SA_HEADS, K_TILE, Q_BLK), F32),
        compiler_params=_cp(("parallel",)),
        name="rel_bias",
    )(rel_bias)


def _dsa_kernel(iq_ref, iwt_ref, ql_ref, ik_ref, c_ref, ct_ref, bt_ref, wuv_ref, o_out,
                key_scr, iq_scr, m_scr, alpha_scr, p_scr, acc_scr, *, topk):
    i = pl.program_id(0)
    t0 = i * Q_BLK
    n_tiles = i // 2 + 1
    H = SA_HEADS
    lane_q = lax.broadcasted_iota(I32, (Q_BLK, LANES), 1)
    first = lane_q < 64

    for p in range(IDX_HEADS // 2):
        pair = iq_ref[:, p * LANES:(p + 1) * LANES]
        swap = pltpu.roll(pair, 64, 1)
        rows = []
        for own, other in ((pair, swap), (swap, pair)):
            o_hi, o_lo = _split2(own)
            t_hi = other.astype(BF16)
            rows.append(jnp.concatenate([jnp.where(first, o_hi, t_hi), jnp.where(first, o_lo, jnp.zeros_like(o_lo))], axis=1))
        iq_scr[p] = jnp.concatenate(rows, axis=0)
    w_rows = [iwt_ref[h:h + 1, :] * IDX_SCALE for h in range(IDX_HEADS)]

    key_pos = lax.broadcasted_iota(I32, (K_TILE, Q_BLK), 0)
    q_lane = lax.broadcasted_iota(I32, (K_TILE, Q_BLK), 1)
    limit = t0 + (q_lane // 64 + 1) * 64

    def score_tiles(jj, carry):
        starts = [pl.multiple_of((2 * jj + u) * K_TILE, K_TILE) for u in range(2)]
        dots = [[_mm_nt(ik_ref[pl.ds(s0, K_TILE), :], iq_scr[p]) for p in range(IDX_HEADS // 2)] for s0 in starts]
        for s0, tile_dots in zip(starts, dots):
            sc = jnp.zeros((K_TILE, Q_BLK), F32)
            for p, d in enumerate(tile_dots):
                sc = (sc + jnp.maximum(d[:, :Q_BLK], 0.0) * w_rows[2 * p]
                      + jnp.maximum(d[:, Q_BLK:], 0.0) * w_rows[2 * p + 1])
            sc = sc + 0.0
            key_scr[pl.ds(s0, K_TILE), :] = jnp.where(s0 + key_pos < limit, sc, -jnp.inf)
        return carry

    lax.fori_loop(0, (n_tiles + 1) // 2, score_tiles, 0)

    n_steps = (n_tiles + R_TILES - 1) // R_TILES

    def pad_tile(j, carry):
        key_scr[pl.ds(pl.multiple_of(j * K_TILE, K_TILE), K_TILE), :] = jnp.full((K_TILE, Q_BLK), -jnp.inf, F32)
        return carry

    lax.fori_loop(n_tiles, n_steps * R_TILES, pad_tile, 0)

    def count(pred):
        def step(js, acc):
            s0 = pl.multiple_of(js * (R_TILES * K_TILE), R_TILES * K_TILE)
            for sub in range(R_TILES * K_TILE // 64):
                hit = jnp.where(pred(key_scr[pl.ds(s0 + sub * 64, 64), :]), 1.0, 0.0)
                acc = acc + jnp.sum(hit.reshape(64 // SUBLANES, SUBLANES, Q_BLK), axis=0)
            return acc
        acc = lax.fori_loop(0, n_steps, step, jnp.zeros((SUBLANES, Q_BLK), F32))
        return jnp.sum(acc, axis=0, keepdims=True)

    def as_float(ordered):
        return pltpu.bitcast(ordered ^ ((ordered >> 31) & 0x7FFFFFFF), F32)

    def bit_step(it, cur):
        bit = lax.shift_left(jnp.int32(1), 31 - it)
        cand = as_float((cur | bit) ^ INT_MIN)
        cnt = count(lambda kt: kt >= cand)
        return jnp.where(cnt >= topk, cur | bit, cur)

    cur = lax.fori_loop(0, 32, bit_step, jnp.zeros((1, Q_BLK), I32))
    take_all = limit[:1, :] <= topk
    theta = jnp.where(take_all, -jnp.inf, as_float(cur ^ INT_MIN))
    need = jnp.where(take_all, 0.0, topk - count(lambda kt: kt > theta))

    lr = lax.broadcasted_iota(I32, (K_TILE, K_TILE), 0)
    lc = lax.broadcasted_iota(I32, (K_TILE, K_TILE), 1)
    lower = jnp.where(lr >= lc, 1.0, 0.0).astype(BF16)

    def mask_tiles(jj, carry):
        starts = [pl.multiple_of((2 * jj + u) * K_TILE, K_TILE) for u in range(2)]
        kts = [key_scr[pl.ds(s0, K_TILE), :] for s0 in starts]
        eqs = [jnp.where(kt == theta, 1.0, 0.0) for kt in kts]
        within = [_mm(lower, e.astype(BF16)) for e in eqs]
        for s0, kt, e, w in zip(starts, kts, eqs, within):
            tie = jnp.where(carry + w <= need, 0.0, NEG_BIG)
            addm = jnp.where(kt > theta, 0.0, jnp.where(kt == theta, tie, NEG_BIG))
            key_scr[pl.ds(s0, K_TILE), :] = addm
            carry = carry + jnp.sum(e, axis=0, keepdims=True)
        return carry

    lax.fori_loop(0, (n_tiles + 1) // 2, mask_tiles, jnp.zeros((1, Q_BLK), F32))

    m_scr[...] = jnp.full(m_scr.shape, NEG_BIG, F32)
    acc_scr[...] = jnp.zeros(acc_scr.shape, F32)
    p_scr[...] = jnp.zeros(p_scr.shape, BF16)
    alpha_scr[...] = jnp.ones(alpha_scr.shape, F32)

    def apply_pending(ct_t, ls):
        acc_scr[:, ls] = alpha_scr[:, ls] * acc_scr[:, ls] + _mm(ct_t, p_scr[:, ls])

    def attn_tile(j, carry, *, near):
        s0 = pl.multiple_of(j * K_TILE, K_TILE)
        c_t = c_ref[pl.ds(s0, K_TILE), :]
        ct_prev = ct_ref[:, pl.ds(pl.multiple_of(jnp.maximum(j - 1, 0) * K_TILE, K_TILE), K_TILE)]
        addm = key_scr[pl.ds(s0, K_TILE), :]
        addm2 = jnp.concatenate([addm, addm], axis=1)

        def pair_logits(p):
            return _mm_nt(c_t, ql_ref[2 * p:2 * p + 2].reshape(2 * Q_BLK, KV_RANK))

        lg2_next = pair_logits(0)
        for p in range(H // 2):
            lg2 = lg2_next
            if p + 1 < H // 2:
                lg2_next = pair_logits(p + 1)
            apply_pending(ct_prev, slice(2 * p * Q_BLK, (2 * p + 2) * Q_BLK))
            for hh in range(2):
                h = 2 * p + hh
                ls = slice(h * Q_BLK, (h + 1) * Q_BLK)
                lg = lg2[:, hh * Q_BLK:(hh + 1) * Q_BLK] + addm
                if near:
                    lg = lg + bt_ref[i - 2 * j, h]
                m_prev = m_scr[:, ls]
                m_new = jnp.maximum(m_prev, jnp.max(lg, axis=0, keepdims=True))
                m_scr[:, ls] = m_new
                alpha_scr[:, ls] = jnp.exp2(m_prev - m_new)
                p_scr[:, ls] = jnp.exp2(lg - m_new).astype(BF16)
        return carry

    n_far = jnp.maximum(i - N_BIAS_TILES + 2, 0) // 2
    lax.fori_loop(0, n_far, functools.partial(attn_tile, near=False), 0)
    lax.fori_loop(n_far, n_tiles, functools.partial(attn_tile, near=True), 0)
    ct_last = ct_ref[:, pl.ds(pl.multiple_of((n_tiles - 1) * K_TILE, K_TILE), K_TILE)]
    for p in range(H // 2):
        apply_pending(ct_last, slice(2 * p * Q_BLK, (2 * p + 2) * Q_BLK))

    o_lat = (acc_scr[:KV_RANK, :] / acc_scr[KV_RANK:KV_RANK + 1, :]).T.astype(BF16)
    for p in range(H // 2):
        pair = jnp.concatenate([o_lat[2 * p * Q_BLK:(2 * p + 1) * Q_BLK], o_lat[(2 * p + 1) * Q_BLK:(2 * p + 2) * Q_BLK]],
                               axis=1)
        o_out[:, p * LANES:(p + 1) * LANES] = _mm(pair, wuv_ref[p])


ONES_ROWS = 16


def _dsa_operands(pp, c):
    ik = pp[:, 896:960]
    ik_hi = ik.astype(BF16)
    ik_lo = (ik - ik_hi.astype(F32)).astype(BF16)
    ik_packed = jnp.concatenate([ik_hi, ik_lo, ik_hi, jnp.zeros_like(ik_hi)], axis=1)
    iwt = jnp.transpose(pp[:, 960:968])
    ct_ext = jnp.concatenate([jnp.transpose(c), jnp.ones((ONES_ROWS, c.shape[0]), c.dtype)], axis=0)
    return iwt, ik_packed, c, ct_ext


def _dsa(pp, iwt, ik_packed, c, ct_ext, ql, bt, wuv2, *, topk):
    T = pp.shape[0]
    H = SA_HEADS
    assert T % (R_TILES * K_TILE) == 0 and N_BIAS_TILES * Q_BLK - (K_TILE - 1) >= _BUCKET_BOUNDS[-1]
    whole = pl.BlockSpec(memory_space=pltpu.VMEM)
    return pl.pallas_call(
        functools.partial(_dsa_kernel, topk=topk),
        grid=(T // Q_BLK,),
        in_specs=[pl.BlockSpec((Q_BLK, 512), lambda i: (i, 0)),
                  pl.BlockSpec((IDX_HEADS, Q_BLK), lambda i: (0, i)),
                  pl.BlockSpec((H, Q_BLK, KV_RANK), lambda i: (0, i, 0)),
                  whole, whole, whole, whole, whole],
        out_specs=pl.BlockSpec((Q_BLK, 1024), lambda i: (i, 0)),
        out_shape=jax.ShapeDtypeStruct((T, 1024), F32),
        scratch_shapes=[pltpu.VMEM((T, Q_BLK), F32),
                        pltpu.VMEM((IDX_HEADS // 2, 2 * Q_BLK, 2 * LANES), BF16),
                        pltpu.VMEM((1, H * Q_BLK), F32),
                        pltpu.VMEM((1, H * Q_BLK), F32),
                        pltpu.VMEM((K_TILE, H * Q_BLK), BF16),
                        pltpu.VMEM((KV_RANK + ONES_ROWS, H * Q_BLK), F32)],
        compiler_params=_cp(("parallel",), 56),
        name="dsa",
    )(pp, iwt, ql, ik_packed, c, ct_ext, bt, wuv2)


def _merge_kernel(y_ref, bonus_ref, g_ref, o_ref, ga_ref, gb_ref, x_ref, lnw_ref, lnb_ref, bd_ref,
                  wa_ref, wb_ref, wout_ref, h_out):
    bd = bd_ref[...]
    wide = bd.shape[0]

    def head_mean(x):
        hi, lo = _split2(x)
        parts = [_mm(hi[:, p * wide:(p + 1) * wide], bd) + _mm(lo[:, p * wide:(p + 1) * wide], bd)
                 for p in range(RW_WIDTH // wide)]
        return jnp.concatenate(parts, axis=1) * (1.0 / 64)

    y = y_ref[...]
    yc = y - head_mean(y)
    yn = yc * lax.rsqrt(head_mean(yc * yc) + RW_GN_EPS) * lnw_ref[...] + lnb_ref[...]
    ya = (yn + bonus_ref[...]) * g_ref[...]
    pa = _mm(ya.astype(BF16), wa_ref[...])
    pb = _mm(o_ref[...].astype(BF16), wb_ref[...])
    m = jax.nn.sigmoid(ga_ref[...]) * pa + jax.nn.sigmoid(gb_ref[...]) * pb
    h_out[...] = x_ref[...] + _mm(m.astype(BF16), wout_ref[...])


def _merge(y, bonus, g, o, pm, x, ln_w, ln_b, bd, wa, wb, wout, *, tm):
    T, D = x.shape
    W = RW_WIDTH
    row = lambda a: a.reshape(1, -1)
    act = pl.BlockSpec((tm, W), lambda i: (i, 0))
    full = lambda shape: pl.BlockSpec(shape, lambda i: (0,) * len(shape))
    whole = pl.BlockSpec(memory_space=pltpu.VMEM)
    return pl.pallas_call(
        _merge_kernel,
        grid=(T // tm,),
        in_specs=[act, act, act, act,
                  pl.BlockSpec((tm, D), lambda i: (i, 0)),
                  pl.BlockSpec((tm, D), lambda i: (i, 1)),
                  pl.BlockSpec((tm, D), lambda i: (i, 0)),
                  full((1, W)), full((1, W)), whole, whole, whole, whole],
        out_specs=pl.BlockSpec((tm, D), lambda i: (i, 0)),
        out_shape=jax.ShapeDtypeStruct((T, D), F32),
        compiler_params=_cp(("parallel",), 56),
        name="merge",
    )(y, bonus, g, o, pm, pm, x, row(ln_w), row(ln_b), bd[:2 * LANES, :2 * LANES], wa, wb, wout)


def _sort16_pairs():
    n, pairs, p = 16, [], 1
    while p < n:
        k = p
        while k >= 1:
            for j in range(k % p, n - k, 2 * k):
                for i in range(min(k, n - j - k)):
                    if (i + j) // (2 * p) == (i + j + k) // (2 * p):
                        pairs.append((i + j, i + j + k))
            k //= 2
        p *= 2
    return pairs


_SORT16 = _sort16_pairs()
PEER_MARGIN = 1.0 - 2.0 ** -20


def _exchange(a, i, j):
    a[i], a[j] = jnp.maximum(a[i], a[j]), jnp.minimum(a[i], a[j])


def _bitonic_sort16(a):
    for dist in (8, 4, 2, 1):
        for i in range(PEER_TOPK):
            if not i & dist:
                _exchange(a, i, i + dist)
    return a


def _top16_merge(x, y):
    return _bitonic_sort16([jnp.maximum(x[i], y[PEER_TOPK - 1 - i]) for i in range(PEER_TOPK)])


def _peer_select_kernel(qp_ref, keys_ref, et_out, tau_out):
    tm = qp_ref.shape[1]
    sub = lax.broadcasted_iota(I32, (SUBLANES, tm), 0)
    zero = jnp.zeros((SUBLANES, tm), F32)

    def per_head(h, packed):
        packed = list(packed)
        for c in range(2):
            hc = 2 * h + c
            s_t = _dot3(keys_ref[hc], qp_ref[hc], _mm_nt)
            e = jnp.exp(s_t - jnp.max(s_t, axis=0, keepdims=True))
            et_out[hc] = e
            a = [e[g * SUBLANES:(g + 1) * SUBLANES, :] for g in range(PEER_KEYS // SUBLANES)]
            for i, j in _SORT16:
                _exchange(a, i, j)
            for shift in (4, 2, 1):
                a = _top16_merge(a, [pltpu.roll(t, shift, 0) for t in a])
            for i in range(PEER_TOPK):
                packed[c * PEER_TOPK + i] = jnp.where(sub == h, a[i], packed[c * PEER_TOPK + i])
        return tuple(packed)

    packed = lax.fori_loop(0, PEER_HEADS, per_head, (zero,) * (2 * PEER_TOPK))
    top_a, top_b = packed[:PEER_TOPK], packed[PEER_TOPK:]

    def run(i, n):
        return [top_a[i] * top_b[j] for j in range(n)]

    g1 = run(0, 16)
    g2 = _bitonic_sort16(run(1, 8) + [top_a[i] * top_b[0] for i in range(PEER_TOPK - 1, 7, -1)])
    g3 = run(2, 5) + run(3, 4) + run(4, 3) + run(5, 2) + run(6, 2)
    for i, j in _SORT16:
        _exchange(g3, i, j)
    top = _top16_merge(_top16_merge(_top16_merge(g1, g2), g3), run(7, 2) + [zero] * 14)
    z = top[0]
    for t in top[1:]:
        z = z + t
    zinv = 1.0 / z
    tau_out[...] = top[PEER_TOPK - 1] * PEER_MARGIN * zinv
    for h in range(PEER_HEADS):
        et_out[2 * h] = et_out[2 * h] * zinv[h:h + 1, :]


def _peer_select(qp, keys, *, tm):
    T = qp.shape[1]
    return pl.pallas_call(
        _peer_select_kernel,
        grid=(T // tm,),
        in_specs=[pl.BlockSpec((2 * PEER_HEADS, tm, LANES), lambda i: (0, i, 0)),
                  pl.BlockSpec((2 * PEER_HEADS, PEER_KEYS, LANES), lambda i: (0, 0, 0))],
        out_specs=[pl.BlockSpec((2 * PEER_HEADS, PEER_KEYS, tm), lambda i: (0, 0, i)),
                   pl.BlockSpec((SUBLANES, tm), lambda i: (0, i))],
        out_shape=[jax.ShapeDtypeStruct((2 * PEER_HEADS, PEER_KEYS, T), F32),
                   jax.ShapeDtypeStruct((PEER_HEADS, T), F32)],
        compiler_params=_cp(("parallel",)),
        name="peer_select",
    )(qp, keys)


def _peer_kernel(h_ref, gf_ref, gout_ref, u_ref, vt_ref, et_ref, tau_ref, out_ref, hn_scr, acc_scr, ga0_scr, ga1_scr,
                 *, e_tile):
    j = pl.program_id(1)
    n_tiles = pl.num_programs(1) - 1
    tm = h_ref.shape[0]
    per_tile = e_tile // PEER_KEYS

    @pl.when(j == 0)
    def _():
        h = h_ref[...]
        hn = h * lax.rsqrt(jnp.mean(h * h, axis=-1, keepdims=True) + EPS) * gf_ref[...]
        hn_scr[...] = hn.astype(BF16)
        acc_scr[...] = jnp.zeros_like(acc_scr)
        ga1_scr[...] = jnp.zeros_like(ga1_scr)

    d_model = vt_ref.shape[0]
    half = e_tile // 2

    def pre_act(s):
        return _mm_nt(u_ref[s * half:(s + 1) * half, :], hn_scr[...])

    def apply_values(q, ga_read):
        rows = slice(q * (d_model // 4), (q + 1) * (d_model // 4))
        acc_scr[rows, :] += _mm(vt_ref[rows, :], ga_read[...])

    RC = 32

    def gated(s, pre, ga_write):
        for ai in range(per_tile // 2):
            a = jnp.minimum(j, n_tiles - 1) * per_tile + s * (per_tile // 2) + ai
            e1 = [et_ref[2 * h, pl.ds(a, 1), :] for h in range(PEER_HEADS)]
            for rc in range(PEER_KEYS // RC):
                x = pre[ai * PEER_KEYS + rc * RC:ai * PEER_KEYS + (rc + 1) * RC, :]
                act = 0.5 * x * (1.0 + lax.erf(x * (2.0 ** -0.5)))
                gate = jnp.zeros((RC, tm), F32)
                for h in range(PEER_HEADS):
                    w = et_ref[2 * h + 1, rc * RC:(rc + 1) * RC, :] * e1[h]
                    gate = gate + jnp.where(w >= tau_ref[h:h + 1, :], w, 0.0)
                r0 = s * half + ai * PEER_KEYS + rc * RC
                ga_write[r0:r0 + RC, :] = (gate * act).astype(BF16)

    def step(ga_write, ga_read):
        pre0 = pre_act(0)
        apply_values(0, ga_read)
        pre1 = pre_act(1)
        gated(0, pre0, ga_write)
        apply_values(1, ga_read)
        apply_values(2, ga_read)
        apply_values(3, ga_read)
        gated(1, pre1, ga_write)

    @pl.when(j % 2 == 0)
    def _():
        step(ga0_scr, ga1_scr)

    @pl.when(j % 2 == 1)
    def _():
        step(ga1_scr, ga0_scr)

    @pl.when(j == n_tiles)
    def _():
        h3 = h_ref[...] + acc_scr[...].T
        out_ref[...] = h3 * lax.rsqrt(jnp.mean(h3 * h3, axis=-1, keepdims=True) + EPS) * gout_ref[...]


def _peer(h1, g_ffn, g_out, u, vt, et, tau, *, tm, e_tile):
    T, D = h1.shape
    E = u.shape[0]
    n_tiles = E // e_tile
    assert n_tiles % 2 == 0
    return pl.pallas_call(
        functools.partial(_peer_kernel, e_tile=e_tile),
        grid=(T // tm, n_tiles + 1),
        in_specs=[pl.BlockSpec((tm, D), lambda i, j: (i, 0)),
                  pl.BlockSpec((1, D), lambda i, j: (0, 0)),
                  pl.BlockSpec((1, D), lambda i, j: (0, 0)),
                  pl.BlockSpec((e_tile, D), lambda i, j: (jnp.minimum(j, n_tiles - 1), 0)),
                  pl.BlockSpec((D, e_tile), lambda i, j: (0, jnp.maximum(j - 1, 0))),
                  pl.BlockSpec((2 * PEER_HEADS, PEER_KEYS, tm), lambda i, j: (0, 0, i)),
                  pl.BlockSpec((PEER_HEADS, tm), lambda i, j: (0, i))],
        out_specs=pl.BlockSpec((tm, D), lambda i, j: (i, 0)),
        out_shape=jax.ShapeDtypeStruct((T, D), F32),
        scratch_shapes=[pltpu.VMEM((tm, D), BF16), pltpu.VMEM((D, tm), F32),
                        pltpu.VMEM((e_tile, tm), BF16), pltpu.VMEM((e_tile, tm), BF16)],
        compiler_params=_cp(("parallel", "arbitrary"), 56),
        name="peer",
    )(h1, g_ffn.reshape(1, D), g_out.reshape(1, D), u, vt, et, tau)


def _pad_cols(a, n):
    return jnp.pad(a, ((0, 0), (0, n - a.shape[1])))


def _block_diag2(a, b):
    za = jnp.zeros((a.shape[0], b.shape[1]), a.dtype)
    zb = jnp.zeros((b.shape[0], a.shape[1]), a.dtype)
    return jnp.concatenate([jnp.concatenate([a, za], axis=1), jnp.concatenate([zb, b], axis=1)], axis=0)


def kernel(x, norm_mix_g, w_in, shift_mix, w0, w_decay_up, a0, w_icl_up, w_gate_up, k_k, k_a, r_k, ln_x_w, ln_x_b,
           kv_norm_g, w_uk, w_uv, rel_bias, w_branch_rwkv, w_branch_dsa, w_out, norm_ffn_g, w_peer_query,
           peer_sub_keys, peer_u, peer_v, norm_final_g):
    B, T, D = x.shape
    assert B == 1 and D == D_MODEL and T % (2 * K_TILE) == 0 and norm_mix_g.shape[0] == 1
    topk = min(TOPK_MAX, T // 4)
    x2 = x[0]
    w = w_in[0]
    sm = shift_mix[0]
    o_zw, o_za, o_zg, o_q, o_ckv, o_iq, o_ik, o_iw, o_g = 3072, 3136, 3200, 3360, 4384, 4640, 5152, 5216, 5224
    w_main = jnp.concatenate([w[:, o_g:o_g + 4096], w[:, :3072], w[:, o_q:o_q + 1024], w[:, o_ckv:o_ckv + 256]],
                             axis=1).astype(BF16)
    w_prec = jnp.concatenate([w[:, o_iq:o_iq + 512], _pad_cols(w[:, o_zg:o_zg + 160], 256), w[:, o_zw:o_zw + 128],
                              _pad_cols(w[:, o_ik:o_ik + 72], 128)], axis=1)
    tm = min(512, T)
    pm = _proj(x2, norm_mix_g[0], w_main, precise=False, tm=min(1024, T), tn=768, name="proj_main")
    pp = _proj(x2, norm_mix_g[0], w_prec, precise=True, tm=tm, tn=512, name="proj_precise")

    head_id = jnp.arange(RW_WIDTH) // 64
    bd = (head_id[:, None] == head_id[None, :]).astype(BF16)
    mixes = [sm[:1024], sm[1024:2048], sm[2048:3072], sm[o_zw:o_zw + 128],
             jnp.pad(sm[o_zg:o_zg + 160], (0, 96))]
    wd = jnp.pad(w_decay_up[0], ((0, 64), (0, 0)))
    wi = jnp.pad(w_icl_up[0], ((64, 0), (0, 0)))
    wg = jnp.pad(w_gate_up[0], ((0, 96), (0, 0)))
    r, k4, v, kkn, b, lw, g, bonus = _rw_prep(pm, pp, mixes, w0[0], wd, a0[0], wi, wg, k_k[0], k_a[0],
                                              r_k[0].reshape(-1), bd, tm=min(256, T))
    nch = T // RW_CHUNK
    p_mat, q_mat, rp, y0 = _rw_chunk(r, k4, v, kkn, b, lw, G=min(8, nch))
    y = _rw_scan(p_mat, q_mat, rp, y0)

    wuk_t = jnp.transpose(w_uk[0], (0, 2, 1))
    wuk2 = jnp.stack([_block_diag2(wuk_t[2 * p], wuk_t[2 * p + 1]) for p in range(SA_HEADS // 2)]).astype(BF16)
    wuv2 = jnp.stack([_block_diag2(w_uv[0][2 * p], w_uv[0][2 * p + 1]) for p in range(SA_HEADS // 2)]).astype(BF16)
    c, ql = _dsa_prep(pm, kv_norm_g[0], wuk2, tm=min(512, T))
    o_dsa = _dsa(pp, *_dsa_operands(pp, c), ql, _rel_bias_tiles(rel_bias), wuv2, topk=topk)

    h1 = _merge(y, bonus, g, o_dsa, pm, x2, ln_x_w[0], ln_x_b[0], bd,
                w_branch_rwkv[0].astype(BF16), w_branch_dsa[0].astype(BF16), w_out[0].astype(BF16),
                tm=min(256, T))

    qp = _proj(h1, norm_ffn_g[0], w_peer_query[0], precise=True, tm=tm, tn=512, name="peer_query", lane_groups=True)
    keys = peer_sub_keys[0].reshape(2 * PEER_HEADS, PEER_KEYS, LANES)
    et, tau = _peer_select(qp, keys, tm=256)
    u = peer_u[0].astype(BF16)
    vt = jnp.transpose(peer_v[0]).astype(BF16)
    out = _peer(h1, norm_ffn_g[0], norm_final_g, u, vt, et, tau, tm=min(512, T), e_tile=512)
    return out[None]
```

```python
import functools
import math

import numpy as np
import jax
import jax.numpy as jnp
from jax import lax
from jax.experimental import pallas as pl
from jax.experimental.pallas import tpu as pltpu

F32, BF16, I32 = jnp.float32, jnp.bfloat16, jnp.int32

D_MODEL = 2048
EPS = 1e-6
RW_WIDTH = 1024
RW_GN_EPS = 64e-5
RW_CHUNK = 128
RW_LOCKSTEP = 8
SA_HEADS = 16
SA_SCALE = 64 ** -0.5
KV_RANK = 256
IDX_HEADS = 8
IDX_SCALE = (8 * 64) ** -0.5
TOPK_MAX = 256
REL_BUCKETS = 32
REL_MAX_DIST = 1024
PEER_HEADS = 8
PEER_KEYS = 128
PEER_TOPK = 16
LANES = 128
SUBLANES = 8
NEG_BIG = -1e30
INT_MIN = -2 ** 31

PM_GATE_A, PM_GATE_B = 0, 1
PM_R, PM_K, PM_V, PM_Q = 4, 5, 6, 7
PM_CKV = 32
PP_ZG = 2
PP_ZWA = 6
PP_IK0, PP_IW0 = 896, 960

Q_BLK = 128
K_TILE = 256
N_BIAS_TILES = 7
R_TILES = 4
LOG2E = math.log2(math.e)


def _cp(sem, vmem_mb=48):
    return pltpu.CompilerParams(dimension_semantics=sem, vmem_limit_bytes=vmem_mb * 1024 * 1024)


def _bf(x):
    return x.astype(BF16)


def _split2(x):
    hi = x.astype(BF16)
    lo = (x - hi.astype(F32)).astype(BF16)
    return hi, lo


def _split2_glue(x):
    hi32 = lax.bitcast_convert_type(lax.bitcast_convert_type(x, jnp.uint32) & jnp.uint32(0xFFFF0000), F32)
    return hi32.astype(BF16), (x - hi32).astype(BF16)


def _mm(a, b):
    return jnp.dot(a, b, preferred_element_type=F32)


def _mm_nt(a, b):
    return lax.dot_general(a, b, (((1,), (1,)), ((), ())), preferred_element_type=F32)


def _mm_tn(a, b):
    return lax.dot_general(a, b, (((0,), (0,)), ((), ())), preferred_element_type=F32)


def _d3(a_s, b_s, mm=_mm):
    return mm(a_s[0], b_s[0]) + (mm(a_s[0], b_s[1]) + mm(a_s[1], b_s[0]))


def _dot3(a, b, mm=_mm):
    return _d3(_split2(a), _split2(b), mm)


def _proj_kernel(*refs, precise, lane_groups):
    if precise:
        x_ref, g_ref, wh_ref, wl_ref, o_ref, xh_ref, xl_ref = refs
    else:
        x_ref, g_ref, wh_ref, o_ref, xh_ref = refs

    @pl.when(pl.program_id(1) == 0)
    def _():
        x = x_ref[...]
        xn = x * lax.rsqrt(jnp.mean(x * x, axis=-1, keepdims=True) + EPS) * g_ref[...]
        hi = xn.astype(BF16)
        xh_ref[...] = hi
        if precise:
            xl_ref[...] = (xn - hi.astype(F32)).astype(BF16)

    if precise:
        xh, wh = xh_ref[...], wh_ref[...]
        res = _mm(xh, wh) + (_mm(xh, wl_ref[...]) + _mm(xl_ref[...], wh))
    else:
        res = _mm(xh_ref[...], wh_ref[...])
    if lane_groups:
        for k in range(res.shape[1] // LANES):
            o_ref[k] = res[:, k * LANES:(k + 1) * LANES]
    else:
        o_ref[...] = res


def _proj(x, g, w, *, precise, tm, tn, name, lane_groups=False):
    T, D = x.shape
    N = w.shape[1]
    weights = list(_split2_glue(w)) if precise else [w]
    if lane_groups:
        out_spec = pl.BlockSpec((tn // LANES, tm, LANES), lambda i, j: (j, i, 0))
        out_shape = jax.ShapeDtypeStruct((N // LANES, T, LANES), F32)
    else:
        out_spec = pl.BlockSpec((tm, tn), lambda i, j: (i, j))
        out_shape = jax.ShapeDtypeStruct((T, N), F32)
    return pl.pallas_call(
        functools.partial(_proj_kernel, precise=precise, lane_groups=lane_groups),
        grid=(T // tm, N // tn),
        in_specs=[pl.BlockSpec((tm, D), lambda i, j: (i, 0)),
                  pl.BlockSpec((1, D), lambda i, j: (0, 0))]
        + [pl.BlockSpec((D, tn), lambda i, j: (0, j))] * len(weights),
        out_specs=out_spec,
        out_shape=out_shape,
        scratch_shapes=[pltpu.VMEM((tm, D), BF16)] * len(weights),
        compiler_params=_cp(("parallel", "arbitrary")),
        name=name,
    )(x, g.reshape(1, D), *weights)


def _rw_prep_kernel(r_ref, k_ref, v_ref, wa_ref, zg_ref, rp_ref, kp_ref, vp_ref, wap_ref, zgp_ref,
                    mr_ref, mk_ref, mv_ref, mwa_ref, mg_ref, w0_ref, wd_ref, a0_ref, wi_ref, wg_ref,
                    kk_ref, ka_ref, rk_ref, bd_ref,
                    r_out, k_out, v_out, kkn_out, b_out, lw_out, g_out, bonus_out):
    first = pl.program_id(0) == 0

    def shift(p_ref, prev_ref, mix_ref):
        p = p_ref[...]
        last = jnp.where(first, 0.0, prev_ref[SUBLANES - 1:SUBLANES, :])
        row = lax.broadcasted_iota(I32, p.shape, 0)
        prev = jnp.where(row == 0, last, pltpu.roll(p, 1, 0))
        return p + (prev - p) * mix_ref[...]

    r = shift(r_ref, rp_ref, mr_ref)
    k = shift(k_ref, kp_ref, mk_ref)
    v = shift(v_ref, vp_ref, mv_ref)
    zwa = shift(wa_ref, wap_ref, mwa_ref)
    zg = shift(zg_ref, zgp_ref, mg_ref)

    nd = -(w0_ref[...] + _dot3(jnp.tanh(zwa), wd_ref[...]))
    softplus = jnp.maximum(nd, 0.0) + jnp.log1p(jnp.exp(-jnp.abs(nd)))
    lw = -jnp.exp(-softplus - 0.5)
    a = jax.nn.sigmoid(a0_ref[...] + _dot3(zwa, wi_ref[...]))
    g = _dot3(jax.nn.sigmoid(zg), wg_ref[...])

    bd = bd_ref[...]
    wide = bd.shape[0]

    def head_sum(x):
        hi, lo = _split2(x)
        parts = [_mm(hi[:, p * wide:(p + 1) * wide], bd) + _mm(lo[:, p * wide:(p + 1) * wide], bd)
                 for p in range(RW_WIDTH // wide)]
        return jnp.concatenate(parts, axis=1)

    kk = k * kk_ref[...]
    kkn = kk / jnp.maximum(jnp.sqrt(head_sum(kk * kk)), 1e-12)
    k4 = k * (1.0 + (a - 1.0) * ka_ref[...])
    r_out[...] = r
    k_out[...] = k4
    v_out[...] = v
    kkn_out[...] = kkn
    b_out[...] = kkn * a
    lw_out[...] = lw
    g_out[...] = g
    bonus_out[...] = head_sum(r * k4 * rk_ref[...]) * v


def _rw_prep(pm, pp, mixes, w0, wd, a0, wi, wg, k_k, k_a, r_k, bd, *, tm):
    T = pm.shape[0]
    W = RW_WIDTH
    pb = tm // SUBLANES

    def cur(width, col):
        return pl.BlockSpec((tm, width), lambda i: (i, col))

    def prev(width, col):
        return pl.BlockSpec((SUBLANES, width), lambda i: (jnp.maximum(i * pb - 1, 0), col))

    def full(shape):
        return pl.BlockSpec(shape, lambda i: (0,) * len(shape))

    row = lambda a: a.reshape(1, -1)
    out_sd = jax.ShapeDtypeStruct((T, W), F32)
    return pl.pallas_call(
        _rw_prep_kernel,
        grid=(T // tm,),
        in_specs=[cur(W, PM_R), cur(W, PM_K), cur(W, PM_V), cur(128, PP_ZWA), cur(256, PP_ZG),
                  prev(W, PM_R), prev(W, PM_K), prev(W, PM_V), prev(128, PP_ZWA), prev(256, PP_ZG),
                  full((1, W)), full((1, W)), full((1, W)), full((1, 128)), full((1, 256)),
                  full((1, W)), full((128, W)), full((1, W)), full((128, W)), full((256, W)),
                  full((1, W)), full((1, W)), full((1, W)), full((2 * LANES, 2 * LANES))],
        out_specs=[pl.BlockSpec((tm, W), lambda i: (i, 0))] * 8,
        out_shape=[out_sd] * 8,
        compiler_params=_cp(("parallel",)),
        name="rw_prep",
    )(pm, pm, pm, pp, pp, pm, pm, pm, pp, pp,
      *[row(m) for m in mixes], row(w0), wd, row(a0), wi, wg, row(k_k), row(k_a), row(r_k), bd[:2 * LANES, :2 * LANES])


def _rw_chunk_kernel(r_ref, k_ref, v_ref, kkn_ref, b_ref, lw_ref, p_out, q_out, rp_out, y0_out, *, G):
    C = RW_CHUNK
    row = lax.broadcasted_iota(I32, (C, C), 0)
    col = lax.broadcasted_iota(I32, (C, C), 1)
    incl = row >= col
    strict = row > col
    eye = row == col
    same_head = (row // 64) == (col // 64)
    tri = jnp.where(incl, 1.0, 0.0).astype(BF16)
    lane = lax.broadcasted_iota(I32, (C, LANES), 1)
    eye_f = jnp.where(eye, 1.0, 0.0)

    head_lanes = (lane < 64, lane >= 64)
    U = RW_LOCKSTEP

    def body(cu, carry):
        ch = [cu * U + u for u in range(U)]
        sls = [pl.ds(pl.multiple_of(c * C, C), C) for c in ch]
        pre = []
        for sl in sls:
            r, k4, v, kkn, b, lw = (ref[sl, :] for ref in (r_ref, k_ref, v_ref, kkn_ref, b_ref, lw_ref))
            h1 = lw.astype(BF16)
            r1 = lw - h1.astype(F32)
            h2 = r1.astype(BF16)
            h3 = (r1 - h2.astype(F32)).astype(BF16)
            L = _mm(tri, h1) + (_mm(tri, h2) + _mm(tri, h3))
            Ltot = L[C - 1:C, :]
            Lr = L - L[C // 2 - 1:C // 2, :]
            eneg = jnp.exp(-Lr)
            tail = jnp.exp(Ltot - L)
            pre.append(dict(
                rhat=r * jnp.exp(Lr), ahat=-kkn * jnp.exp(Lr - lw), kt=_bf(k4 * eneg), bt=_bf(b * eneg),
                rabs=r * jnp.exp(L), aabs=-kkn * jnp.exp(L - lw), kbar=_bf(k4 * tail), bbar=_bf(b * tail),
                gam=jnp.exp(Ltot), vb=_bf(v)))
        inst = [(u, hh) for u in range(U) for hh in range(2)]
        am = [_bf(jnp.where(head_lanes[hh], pre[u]["ahat"], 0.0)) for u, hh in inst]
        rm = [_bf(jnp.where(head_lanes[hh], pre[u]["rhat"], 0.0)) for u, hh in inst]
        a_ab = [jnp.where(strict, _mm_nt(am[k], pre[u]["bt"]), 0.0) for k, (u, hh) in enumerate(inst)]
        a_ak = [_bf(jnp.where(strict, _mm_nt(am[k], pre[u]["kt"]), 0.0)) for k, (u, hh) in enumerate(inst)]
        m_rb = [_bf(jnp.where(incl, _mm_nt(rm[k], pre[u]["bt"]), 0.0)) for k, (u, hh) in enumerate(inst)]
        m_rk = [_bf(jnp.where(incl, _mm_nt(rm[k], pre[u]["kt"]), 0.0)) for k, (u, hh) in enumerate(inst)]
        n = a_ab
        x = [eye_f + t for t in n]
        for _ in range(6):
            nb = [_bf(t) for t in n]
            n = [_mm(t, t) for t in nb]
            x = [xi + _mm(_bf(xi), _bf(ni)) for xi, ni in zip(x, n)]
        xb = [_bf(t) for t in x]
        akv = [_bf(_mm(a_ak[k], pre[u]["vb"])) for k, (u, hh) in enumerate(inst)]
        ah_h = [_mm(xb[k], _bf(jnp.where(head_lanes[hh], pre[u]["aabs"], 0.0))) for k, (u, hh) in enumerate(inst)]
        u0_h = [_mm(xb[k], akv[k]) for k in range(len(inst))]
        y0_h = [_mm(m_rk[k], pre[u]["vb"]) + _mm(m_rb[k], _bf(u0_h[k])) for k, (u, hh) in enumerate(inst)]
        rp_h = [jnp.where(head_lanes[hh], pre[u]["rabs"], 0.0) + _mm(m_rb[k], _bf(ah_h[k]))
                for k, (u, hh) in enumerate(inst)]
        for u in range(U):
            lo, hi = 2 * u, 2 * u + 1
            ah = _bf(ah_h[lo] + ah_h[hi])
            u0 = _bf(jnp.where(head_lanes[0], u0_h[lo], u0_h[hi]))
            pu = pre[u]
            p_out[0, ch[u]] = jnp.where(eye, pu["gam"], 0.0) + jnp.where(same_head, _mm_tn(pu["bbar"], ah), 0.0)
            q_out[0, ch[u]] = jnp.where(same_head, _mm_tn(pu["kbar"], pu["vb"]) + _mm_tn(pu["bbar"], u0), 0.0)
            rp_out[sls[u], :] = rp_h[lo] + rp_h[hi]
            y0_out[sls[u], :] = jnp.where(head_lanes[0], y0_h[lo], y0_h[hi])
        return carry

    lax.fori_loop(0, G // U, body, 0)


def _rw_chunk(r, k4, v, kkn, b, lw, *, G):
    T = r.shape[0]
    C = RW_CHUNK
    nch = T // C
    rows = G * C
    cur = pl.BlockSpec((rows, LANES), lambda p, g: (g, p))
    mat = pl.BlockSpec((1, G, C, C), lambda p, g: (p, g, 0, 0))
    return pl.pallas_call(
        functools.partial(_rw_chunk_kernel, G=G),
        grid=(RW_WIDTH // LANES, nch // G),
        in_specs=[cur] * 6,
        out_specs=[mat, mat, cur, cur],
        out_shape=[jax.ShapeDtypeStruct((RW_WIDTH // LANES, nch, C, C), F32)] * 2
        + [jax.ShapeDtypeStruct((T, RW_WIDTH), F32)] * 2,
        compiler_params=_cp(("parallel", "parallel")),
        name="rw_chunk",
    )(r, k4, v, kkn, b, lw)


def _rw_scan_kernel(p_ref, q_ref, rp_ref, y0_ref, y_out, s_scr):
    @pl.when(pl.program_id(0) == 0)
    def _():
        s_scr[...] = jnp.zeros_like(s_scr)

    for p in range(RW_WIDTH // LANES):
        ls = slice(p * LANES, (p + 1) * LANES)
        s0 = _bf(s_scr[p])
        y_out[:, ls] = _mm(_bf(rp_ref[:, ls]), s0) + y0_ref[:, ls]
        s_scr[p] = _mm(_bf(p_ref[p, 0]), s0) + q_ref[p, 0]


def _rw_scan(p_mat, q_mat, rp, y0):
    T = rp.shape[0]
    C = RW_CHUNK
    npair = RW_WIDTH // LANES
    mat = pl.BlockSpec((npair, 1, C, C), lambda c: (0, c, 0, 0))
    cur = pl.BlockSpec((C, RW_WIDTH), lambda c: (c, 0))
    return pl.pallas_call(
        _rw_scan_kernel,
        grid=(T // C,),
        in_specs=[mat, mat, cur, cur],
        out_specs=cur,
        out_shape=jax.ShapeDtypeStruct((T, RW_WIDTH), F32),
        scratch_shapes=[pltpu.VMEM((npair, C, C), F32)],
        compiler_params=_cp(("arbitrary",)),
        name="rw_scan",
    )(p_mat, q_mat, rp, y0)


def _dsa_prep_kernel(q_ref, ckv_ref, g_ref, wuk_ref, c_out, ql_out):
    ckv = ckv_ref[...]
    c = ckv * lax.rsqrt(jnp.mean(ckv * ckv, axis=-1, keepdims=True) + EPS) * g_ref[...]
    c_out[...] = c.astype(BF16)
    for p in range(SA_HEADS // 2):
        qp = q_ref[:, p * LANES:(p + 1) * LANES].astype(BF16)
        ql = _mm(qp, wuk_ref[p]) * (SA_SCALE * LOG2E)
        ql_out[2 * p] = ql[:, :KV_RANK].astype(BF16)
        ql_out[2 * p + 1] = ql[:, KV_RANK:].astype(BF16)


def _dsa_prep(pm, kv_norm_g, wuk2, *, tm):
    T = pm.shape[0]
    return pl.pallas_call(
        _dsa_prep_kernel,
        grid=(T // tm,),
        in_specs=[pl.BlockSpec((tm, 1024), lambda i: (i, PM_Q)),
                  pl.BlockSpec((tm, KV_RANK), lambda i: (i, PM_CKV)),
                  pl.BlockSpec((1, KV_RANK), lambda i: (0, 0)),
                  pl.BlockSpec((SA_HEADS // 2, LANES, 2 * KV_RANK), lambda i: (0, 0, 0))],
        out_specs=[pl.BlockSpec((tm, KV_RANK), lambda i: (i, 0)),
                   pl.BlockSpec((SA_HEADS, tm, KV_RANK), lambda i: (0, i, 0))],
        out_shape=[jax.ShapeDtypeStruct((T, KV_RANK), BF16),
                   jax.ShapeDtypeStruct((SA_HEADS, T, KV_RANK), BF16)],
        compiler_params=_cp(("parallel",)),
        name="dsa_prep",
    )(pm, pm, kv_norm_g.reshape(1, KV_RANK), wuk2)


def _bucket_bounds():
    nb = REL_BUCKETS // 2
    max_exact = nb // 2
    n = np.arange(1, 2 * REL_MAX_DIST, dtype=np.int64)
    large = max_exact + (np.log(n.astype(np.float32) / max_exact)
                         / math.log(REL_MAX_DIST / max_exact) * (nb - max_exact)).astype(np.int32)
    large = np.minimum(large, nb - 1)
    bounds = [int(n[np.argmax(large >= b)]) for b in range(max_exact + 1, nb)]
    assert all(b2 > b1 for b1, b2 in zip(bounds, bounds[1:]))
    return bounds


_BUCKET_BOUNDS = _bucket_bounds()


def _rel_bias_kernel(rb_ref, bt_out):
    o = pl.program_id(0)
    sl = lax.broadcasted_iota(I32, (K_TILE, Q_BLK), 0)
    tl = lax.broadcasted_iota(I32, (K_TILE, Q_BLK), 1)
    rel = sl - tl - o * Q_BLK
    nb = REL_BUCKETS // 2
    max_exact = nb // 2
    n = jnp.abs(rel)
    large = jnp.full(rel.shape, max_exact, I32)
    for bound in _BUCKET_BOUNDS:
        large = large + jnp.where(n >= bound, 1, 0)
    bucket = jnp.where(rel > 0, nb, 0) + jnp.where(n < max_exact, n, large)
    for h in range(SA_HEADS):
        acc = jnp.zeros(rel.shape, F32)
        for bkt in range(REL_BUCKETS):
            acc = jnp.where(bucket == bkt, rb_ref[bkt, h], acc)
        bt_out[0, h] = (acc - rb_ref[nb - 1, h]) * LOG2E


def _rel_bias_tiles(rel_bias):
    return pl.pallas_call(
        _rel_bias_kernel,
        grid=(N_BIAS_TILES,),
        in_specs=[pl.BlockSpec(memory_space=pltpu.SMEM)],
        out_specs=pl.BlockSpec((1, SA_HEADS, K_TILE, Q_BLK), lambda o: (o, 0, 0, 0)),
        out_shape=jax.ShapeDtypeStruct((N_BIAS_TILES, SA_HEADS, K_TILE, Q_BLK), F32),
        compiler_params=_cp(("parallel",)),
        name="rel_bias",
    )(rel_bias)


def _dsa_kernel(iq_ref, iwt_ref, ql_ref, ik_ref, c_ref, ct_ref, bt_ref, wuv_ref, o_out,
                key_scr, iq_scr, m_scr, alpha_scr, p_scr, acc_scr, *, topk):
    i = pl.program_id(0)
    t0 = i * Q_BLK
    n_tiles = i // 2 + 1
    H = SA_HEADS
    lane_q = lax.broadcasted_iota(I32, (Q_BLK, LANES), 1)
    first = lane_q < 64

    for p in range(IDX_HEADS // 2):
        pair = iq_ref[:, p * LANES:(p + 1) * LANES]
        swap = pltpu.roll(pair, 64, 1)
        rows = []
        for own, other in ((pair, swap), (swap, pair)):
            o_hi, o_lo = _split2(own)
            t_hi = other.astype(BF16)
            rows.append(jnp.concatenate([jnp.where(first, o_hi, t_hi), jnp.where(first, o_lo, jnp.zeros_like(o_lo))], axis=1))
        iq_scr[p] = jnp.concatenate(rows, axis=0)
    w_rows = [iwt_ref[h:h + 1, :] * IDX_SCALE for h in range(IDX_HEADS)]

    key_pos = lax.broadcasted_iota(I32, (K_TILE, Q_BLK), 0)
    q_lane = lax.broadcasted_iota(I32, (K_TILE, Q_BLK), 1)
    limit = t0 + (q_lane // 64 + 1) * 64

    def score_tiles(jj, carry):
        starts = [pl.multiple_of((2 * jj + u) * K_TILE, K_TILE) for u in range(2)]
        dots = [[_mm_nt(ik_ref[pl.ds(s0, K_TILE), :], iq_scr[p]) for p in range(IDX_HEADS // 2)] for s0 in starts]
        for s0, tile_dots in zip(starts, dots):
            sc = jnp.zeros((K_TILE, Q_BLK), F32)
            for p, d in enumerate(tile_dots):
                sc = (sc + jnp.maximum(d[:, :Q_BLK], 0.0) * w_rows[2 * p]
                      + jnp.maximum(d[:, Q_BLK:], 0.0) * w_rows[2 * p + 1])
            sc = sc + 0.0
            key_scr[pl.ds(s0, K_TILE), :] = jnp.where(s0 + key_pos < limit, sc, -jnp.inf)
        return carry

    lax.fori_loop(0, (n_tiles + 1) // 2, score_tiles, 0)

    n_steps = (n_tiles + R_TILES - 1) // R_TILES

    def pad_tile(j, carry):
        key_scr[pl.ds(pl.multiple_of(j * K_TILE, K_TILE), K_TILE), :] = jnp.full((K_TILE, Q_BLK), -jnp.inf, F32)
        return carry

    lax.fori_loop(n_tiles, n_steps * R_TILES, pad_tile, 0)

    def count(pred):
        def step(js, acc):
            s0 = pl.multiple_of(js * (R_TILES * K_TILE), R_TILES * K_TILE)
            for sub in range(R_TILES * K_TILE // 64):
                hit = jnp.where(pred(key_scr[pl.ds(s0 + sub * 64, 64), :]), 1.0, 0.0)
                acc = acc + jnp.sum(hit.reshape(64 // SUBLANES, SUBLANES, Q_BLK), axis=0)
            return acc
        acc = lax.fori_loop(0, n_steps, step, jnp.zeros((SUBLANES, Q_BLK), F32))
        return jnp.sum(acc, axis=0, keepdims=True)

    def as_float(ordered):
        return pltpu.bitcast(ordered ^ ((ordered >> 31) & 0x7FFFFFFF), F32)

    def bit_step(it, cur):
        bit = lax.shift_left(jnp.int32(1), 31 - it)
        cand = as_float((cur | bit) ^ INT_MIN)
        cnt = count(lambda kt: kt >= cand)
        return jnp.where(cnt >= topk, cur | bit, cur)

    cur = lax.fori_loop(0, 32, bit_step, jnp.zeros((1, Q_BLK), I32))
    take_all = limit[:1, :] <= topk
    theta = jnp.where(take_all, -jnp.inf, as_float(cur ^ INT_MIN))
    need = jnp.where(take_all, 0.0, topk - count(lambda kt: kt > theta))

    lr = lax.broadcasted_iota(I32, (K_TILE, K_TILE), 0)
    lc = lax.broadcasted_iota(I32, (K_TILE, K_TILE), 1)
    lower = jnp.where(lr >= lc, 1.0, 0.0).astype(BF16)

    def mask_tiles(jj, carry):
        starts = [pl.multiple_of((2 * jj + u) * K_TILE, K_TILE) for u in range(2)]
        kts = [key_scr[pl.ds(s0, K_TILE), :] for s0 in starts]
        eqs = [jnp.where(kt == theta, 1.0, 0.0) for kt in kts]
        within = [_mm(lower, e.astype(BF16)) for e in eqs]
        for s0, kt, e, w in zip(starts, kts, eqs, within):
            tie = jnp.where(carry + w <= need, 0.0, NEG_BIG)
            addm = jnp.where(kt > theta, 0.0, jnp.where(kt == theta, tie, NEG_BIG))
            key_scr[pl.ds(s0, K_TILE), :] = addm
            carry = carry + jnp.sum(e, axis=0, keepdims=True)
        return carry

    lax.fori_loop(0, (n_tiles + 1) // 2, mask_tiles, jnp.zeros((1, Q_BLK), F32))

    m_scr[...] = jnp.full(m_scr.shape, NEG_BIG, F32)
    acc_scr[...] = jnp.zeros(acc_scr.shape, F32)
    p_scr[...] = jnp.zeros(p_scr.shape, BF16)
    alpha_scr[...] = jnp.ones(alpha_scr.shape, F32)

    def apply_pending(ct_t, ls):
        acc_scr[:, ls] = alpha_scr[:, ls] * acc_scr[:, ls] + _mm(ct_t, p_scr[:, ls])

    def attn_tile(j, carry, *, near):
        s0 = pl.multiple_of(j * K_TILE, K_TILE)
        c_t = c_ref[pl.ds(s0, K_TILE), :]
        ct_prev = ct_ref[:, pl.ds(pl.multiple_of(jnp.maximum(j - 1, 0) * K_TILE, K_TILE), K_TILE)]
        addm = key_scr[pl.ds(s0, K_TILE), :]
        addm2 = jnp.concatenate([addm, addm], axis=1)

        def pair_logits(p):
            return _mm_nt(c_t, ql_ref[2 * p:2 * p + 2].reshape(2 * Q_BLK, KV_RANK))

        lg2_next = pair_logits(0)
        for p in range(H // 2):
            lg2 = lg2_next
            if p + 1 < H // 2:
                lg2_next = pair_logits(p + 1)
            apply_pending(ct_prev, slice(2 * p * Q_BLK, (2 * p + 2) * Q_BLK))
            for hh in range(2):
                h = 2 * p + hh
                ls = slice(h * Q_BLK, (h + 1) * Q_BLK)
                lg = lg2[:, hh * Q_BLK:(hh + 1) * Q_BLK] + addm
                if near:
                    lg = lg + bt_ref[i - 2 * j, h]
                m_prev = m_scr[:, ls]
                m_new = jnp.maximum(m_prev, jnp.max(lg, axis=0, keepdims=True))
                m_scr[:, ls] = m_new
                alpha_scr[:, ls] = jnp.exp2(m_prev - m_new)
                p_scr[:, ls] = jnp.exp2(lg - m_new).astype(BF16)
        return carry

    n_far = jnp.maximum(i - N_BIAS_TILES + 2, 0) // 2
    lax.fori_loop(0, n_far, functools.partial(attn_tile, near=False), 0)
    lax.fori_loop(n_far, n_tiles, functools.partial(attn_tile, near=True), 0)
    ct_last = ct_ref[:, pl.ds(pl.multiple_of((n_tiles - 1) * K_TILE, K_TILE), K_TILE)]
    for p in range(H // 2):
        apply_pending(ct_last, slice(2 * p * Q_BLK, (2 * p + 2) * Q_BLK))

    o_lat = (acc_scr[:KV_RANK, :] / acc_scr[KV_RANK:KV_RANK + 1, :]).T.astype(BF16)
    for p in range(H // 2):
        pair = jnp.concatenate([o_lat[2 * p * Q_BLK:(2 * p + 1) * Q_BLK], o_lat[(2 * p + 1) * Q_BLK:(2 * p + 2) * Q_BLK]],
                               axis=1)
        o_out[:, p * LANES:(p + 1) * LANES] = _mm(pair, wuv_ref[p])


ONES_ROWS = 16


def _dsa_operands(pp, c):
    ik = pp[:, PP_IK0:PP_IK0 + 64]
    ik_hi, ik_lo = _split2_glue(ik)
    ik_packed = jnp.concatenate([ik_hi, ik_lo, ik_hi, jnp.zeros_like(ik_hi)], axis=1)
    iwt = jnp.transpose(pp[:, PP_IW0:PP_IW0 + IDX_HEADS])
    ct_ext = jnp.concatenate([jnp.transpose(c), jnp.ones((ONES_ROWS, c.shape[0]), c.dtype)], axis=0)
    return iwt, ik_packed, c, ct_ext


def _dsa(pp, iwt, ik_packed, c, ct_ext, ql, bt, wuv2, *, topk):
    T = pp.shape[0]
    H = SA_HEADS
    assert T % (R_TILES * K_TILE) == 0 and N_BIAS_TILES * Q_BLK - (K_TILE - 1) >= _BUCKET_BOUNDS[-1]
    whole = pl.BlockSpec(memory_space=pltpu.VMEM)
    return pl.pallas_call(
        functools.partial(_dsa_kernel, topk=topk),
        grid=(T // Q_BLK,),
        in_specs=[pl.BlockSpec((Q_BLK, 512), lambda i: (i, 0)),
                  pl.BlockSpec((IDX_HEADS, Q_BLK), lambda i: (0, i)),
                  pl.BlockSpec((H, Q_BLK, KV_RANK), lambda i: (0, i, 0)),
                  whole, whole, whole, whole, whole],
        out_specs=pl.BlockSpec((Q_BLK, 1024), lambda i: (i, 0)),
        out_shape=jax.ShapeDtypeStruct((T, 1024), F32),
        scratch_shapes=[pltpu.VMEM((T, Q_BLK), F32),
                        pltpu.VMEM((IDX_HEADS // 2, 2 * Q_BLK, 2 * LANES), BF16),
                        pltpu.VMEM((1, H * Q_BLK), F32),
                        pltpu.VMEM((1, H * Q_BLK), F32),
                        pltpu.VMEM((K_TILE, H * Q_BLK), BF16),
                        pltpu.VMEM((KV_RANK + ONES_ROWS, H * Q_BLK), F32)],
        compiler_params=_cp(("parallel",), 56),
        name="dsa",
    )(pp, iwt, ql, ik_packed, c, ct_ext, bt, wuv2)


def _merge_kernel(y_ref, bonus_ref, g_ref, o_ref, ga_ref, gb_ref, x_ref, lnw_ref, lnb_ref, bd_ref,
                  wa_ref, wb_ref, wout_ref, h_out):
    bd = bd_ref[...]
    wide = bd.shape[0]

    def head_mean(x):
        hi, lo = _split2(x)
        parts = [_mm(hi[:, p * wide:(p + 1) * wide], bd) + _mm(lo[:, p * wide:(p + 1) * wide], bd)
                 for p in range(RW_WIDTH // wide)]
        return jnp.concatenate(parts, axis=1) * (1.0 / 64)

    y = y_ref[...]
    yc = y - head_mean(y)
    yn = yc * lax.rsqrt(head_mean(yc * yc) + RW_GN_EPS) * lnw_ref[...] + lnb_ref[...]
    ya = (yn + bonus_ref[...]) * g_ref[...]
    pa = _mm(ya.astype(BF16), wa_ref[...])
    pb = _mm(o_ref[...].astype(BF16), wb_ref[...])
    m = jax.nn.sigmoid(ga_ref[...]) * pa + jax.nn.sigmoid(gb_ref[...]) * pb
    h_out[...] = x_ref[...] + _mm(m.astype(BF16), wout_ref[...])


def _merge(y, bonus, g, o, pm, x, ln_w, ln_b, bd, wa, wb, wout, *, tm):
    T, D = x.shape
    W = RW_WIDTH
    row = lambda a: a.reshape(1, -1)
    act = pl.BlockSpec((tm, W), lambda i: (i, 0))
    full = lambda shape: pl.BlockSpec(shape, lambda i: (0,) * len(shape))
    whole = pl.BlockSpec(memory_space=pltpu.VMEM)
    return pl.pallas_call(
        _merge_kernel,
        grid=(T // tm,),
        in_specs=[act, act, act, act,
                  pl.BlockSpec((tm, D), lambda i: (i, PM_GATE_A)),
                  pl.BlockSpec((tm, D), lambda i: (i, PM_GATE_B)),
                  pl.BlockSpec((tm, D), lambda i: (i, 0)),
                  full((1, W)), full((1, W)), whole, whole, whole, whole],
        out_specs=pl.BlockSpec((tm, D), lambda i: (i, 0)),
        out_shape=jax.ShapeDtypeStruct((T, D), F32),
        compiler_params=_cp(("parallel",), 56),
        name="merge",
    )(y, bonus, g, o, pm, pm, x, row(ln_w), row(ln_b), bd[:2 * LANES, :2 * LANES], wa, wb, wout)


def _sort16_pairs():
    n, pairs, p = 16, [], 1
    while p < n:
        k = p
        while k >= 1:
            for j in range(k % p, n - k, 2 * k):
                for i in range(min(k, n - j - k)):
                    if (i + j) // (2 * p) == (i + j + k) // (2 * p):
                        pairs.append((i + j, i + j + k))
            k //= 2
        p *= 2
    return pairs


_SORT16 = _sort16_pairs()
PEER_MARGIN = 1.0 - 2.0 ** -20


def _exchange(a, i, j):
    a[i], a[j] = jnp.maximum(a[i], a[j]), jnp.minimum(a[i], a[j])


def _bitonic_sort16(a):
    for dist in (8, 4, 2, 1):
        for i in range(PEER_TOPK):
            if not i & dist:
                _exchange(a, i, i + dist)
    return a


def _top16_merge(x, y):
    return _bitonic_sort16([jnp.maximum(x[i], y[PEER_TOPK - 1 - i]) for i in range(PEER_TOPK)])


def _peer_select_kernel(qp_ref, keys_ref, et_out, tau_out):
    tm = qp_ref.shape[1]
    sub = lax.broadcasted_iota(I32, (SUBLANES, tm), 0)
    zero = jnp.zeros((SUBLANES, tm), F32)

    def per_head(h, packed):
        packed = list(packed)
        for c in range(2):
            hc = 2 * h + c
            s_t = _dot3(keys_ref[hc], qp_ref[hc], _mm_nt)
            e = jnp.exp(s_t - jnp.max(s_t, axis=0, keepdims=True))
            et_out[hc] = e
            a = [e[g * SUBLANES:(g + 1) * SUBLANES, :] for g in range(PEER_KEYS // SUBLANES)]
            for i, j in _SORT16:
                _exchange(a, i, j)
            for shift in (4, 2, 1):
                a = _top16_merge(a, [pltpu.roll(t, shift, 0) for t in a])
            for i in range(PEER_TOPK):
                packed[c * PEER_TOPK + i] = jnp.where(sub == h, a[i], packed[c * PEER_TOPK + i])
        return tuple(packed)

    packed = lax.fori_loop(0, PEER_HEADS, per_head, (zero,) * (2 * PEER_TOPK))
    top_a, top_b = packed[:PEER_TOPK], packed[PEER_TOPK:]

    def run(i, n):
        return [top_a[i] * top_b[j] for j in range(n)]

    g1 = run(0, 16)
    g2 = _bitonic_sort16(run(1, 8) + [top_a[i] * top_b[0] for i in range(PEER_TOPK - 1, 7, -1)])
    g3 = run(2, 5) + run(3, 4) + run(4, 3) + run(5, 2) + run(6, 2)
    for i, j in _SORT16:
        _exchange(g3, i, j)
    top = _top16_merge(_top16_merge(_top16_merge(g1, g2), g3), run(7, 2) + [zero] * 14)
    z = top[0]
    for t in top[1:]:
        z = z + t
    zinv = 1.0 / z
    tau_out[...] = top[PEER_TOPK - 1] * PEER_MARGIN * zinv
    for h in range(PEER_HEADS):
        et_out[2 * h] = et_out[2 * h] * zinv[h:h + 1, :]


def _peer_select(qp, keys, *, tm):
    T = qp.shape[1]
    return pl.pallas_call(
        _peer_select_kernel,
        grid=(T // tm,),
        in_specs=[pl.BlockSpec((2 * PEER_HEADS, tm, LANES), lambda i: (0, i, 0)),
                  pl.BlockSpec((2 * PEER_HEADS, PEER_KEYS, LANES), lambda i: (0, 0, 0))],
        out_specs=[pl.BlockSpec((2 * PEER_HEADS, PEER_KEYS, tm), lambda i: (0, 0, i)),
                   pl.BlockSpec((SUBLANES, tm), lambda i: (0, i))],
        out_shape=[jax.ShapeDtypeStruct((2 * PEER_HEADS, PEER_KEYS, T), F32),
                   jax.ShapeDtypeStruct((PEER_HEADS, T), F32)],
        compiler_params=_cp(("parallel",)),
        name="peer_select",
    )(qp, keys)


def _peer_kernel(h_ref, gf_ref, gout_ref, u_ref, vt_ref, et_ref, tau_ref, out_ref, hn_scr, acc_scr, ga0_scr, ga1_scr,
                 *, e_tile):
    j = pl.program_id(1)
    n_tiles = pl.num_programs(1) - 1
    tm = h_ref.shape[0]
    per_tile = e_tile // PEER_KEYS

    @pl.when(j == 0)
    def _():
        h = h_ref[...]
        hn = h * lax.rsqrt(jnp.mean(h * h, axis=-1, keepdims=True) + EPS) * gf_ref[...]
        hn_scr[...] = hn.astype(BF16)
        acc_scr[...] = jnp.zeros_like(acc_scr)
        ga1_scr[...] = jnp.zeros_like(ga1_scr)

    d_model = vt_ref.shape[0]
    half = e_tile // 2

    def pre_act(s):
        return _mm_nt(u_ref[s * half:(s + 1) * half, :], hn_scr[...])

    def apply_values(q, ga_read):
        rows = slice(q * (d_model // 4), (q + 1) * (d_model // 4))
        acc_scr[rows, :] += _mm(vt_ref[rows, :], ga_read[...])

    RC = 32

    def gated(s, pre, ga_write):
        for ai in range(per_tile // 2):
            a = jnp.minimum(j, n_tiles - 1) * per_tile + s * (per_tile // 2) + ai
            e1 = [et_ref[2 * h, pl.ds(a, 1), :] for h in range(PEER_HEADS)]
            for rc in range(PEER_KEYS // RC):
                x = pre[ai * PEER_KEYS + rc * RC:ai * PEER_KEYS + (rc + 1) * RC, :]
                act = 0.5 * x * (1.0 + lax.erf(x * (2.0 ** -0.5)))
                gate = jnp.zeros((RC, tm), F32)
                for h in range(PEER_HEADS):
                    w = et_ref[2 * h + 1, rc * RC:(rc + 1) * RC, :] * e1[h]
                    gate = gate + jnp.where(w >= tau_ref[h:h + 1, :], w, 0.0)
                r0 = s * half + ai * PEER_KEYS + rc * RC
                ga_write[r0:r0 + RC, :] = (gate * act).astype(BF16)

    def step(ga_write, ga_read):
        pre0 = pre_act(0)
        apply_values(0, ga_read)
        pre1 = pre_act(1)
        gated(0, pre0, ga_write)
        apply_values(1, ga_read)
        apply_values(2, ga_read)
        apply_values(3, ga_read)
        gated(1, pre1, ga_write)

    @pl.when(j % 2 == 0)
    def _():
        step(ga0_scr, ga1_scr)

    @pl.when(j % 2 == 1)
    def _():
        step(ga1_scr, ga0_scr)

    @pl.when(j == n_tiles)
    def _():
        h3 = h_ref[...] + acc_scr[...].T
        out_ref[...] = h3 * lax.rsqrt(jnp.mean(h3 * h3, axis=-1, keepdims=True) + EPS) * gout_ref[...]


def _peer(h1, g_ffn, g_out, u, vt, et, tau, *, tm, e_tile):
    T, D = h1.shape
    E = u.shape[0]
    n_tiles = E // e_tile
    assert n_tiles % 2 == 0
    return pl.pallas_call(
        functools.partial(_peer_kernel, e_tile=e_tile),
        grid=(T // tm, n_tiles + 1),
        in_specs=[pl.BlockSpec((tm, D), lambda i, j: (i, 0)),
                  pl.BlockSpec((1, D), lambda i, j: (0, 0)),
                  pl.BlockSpec((1, D), lambda i, j: (0, 0)),
                  pl.BlockSpec((e_tile, D), lambda i, j: (jnp.minimum(j, n_tiles - 1), 0)),
                  pl.BlockSpec((D, e_tile), lambda i, j: (0, jnp.maximum(j - 1, 0))),
                  pl.BlockSpec((2 * PEER_HEADS, PEER_KEYS, tm), lambda i, j: (0, 0, i)),
                  pl.BlockSpec((PEER_HEADS, tm), lambda i, j: (0, i))],
        out_specs=pl.BlockSpec((tm, D), lambda i, j: (i, 0)),
        out_shape=jax.ShapeDtypeStruct((T, D), F32),
        scratch_shapes=[pltpu.VMEM((tm, D), BF16), pltpu.VMEM((D, tm), F32),
                        pltpu.VMEM((e_tile, tm), BF16), pltpu.VMEM((e_tile, tm), BF16)],
        compiler_params=_cp(("parallel", "arbitrary"), 56),
        name="peer",
    )(h1, g_ffn.reshape(1, D), g_out.reshape(1, D), u, vt, et, tau)


def _pad_cols(a, n):
    return jnp.pad(a, ((0, 0), (0, n - a.shape[1])))


def _block_diag2(a, b):
    za = jnp.zeros((a.shape[0], b.shape[1]), a.dtype)
    zb = jnp.zeros((b.shape[0], a.shape[1]), a.dtype)
    return jnp.concatenate([jnp.concatenate([a, za], axis=1), jnp.concatenate([zb, b], axis=1)], axis=0)


def kernel(x, norm_mix_g, w_in, shift_mix, w0, w_decay_up, a0, w_icl_up, w_gate_up, k_k, k_a, r_k, ln_x_w, ln_x_b,
           kv_norm_g, w_uk, w_uv, rel_bias, w_branch_rwkv, w_branch_dsa, w_out, norm_ffn_g, w_peer_query,
           peer_sub_keys, peer_u, peer_v, norm_final_g):
    B, T, D = x.shape
    assert B == 1 and D == D_MODEL and T % (2 * K_TILE) == 0 and norm_mix_g.shape[0] == 1
    topk = min(TOPK_MAX, T // 4)
    x2 = x[0]
    w = w_in[0]
    sm = shift_mix[0]
    o_zw, o_za, o_zg, o_q, o_ckv, o_iq, o_ik, o_iw, o_g = 3072, 3136, 3200, 3360, 4384, 4640, 5152, 5216, 5224
    wb16 = w.astype(BF16)
    w_main = jnp.concatenate([wb16[:, o_g:o_g + 4096], wb16[:, :3072], wb16[:, o_q:o_q + 1024],
                              wb16[:, o_ckv:o_ckv + 256]], axis=1)
    w_prec = jnp.concatenate([w[:, o_iq:o_iq + 512], _pad_cols(w[:, o_zg:o_zg + 160], 256), w[:, o_zw:o_zw + 128],
                              _pad_cols(w[:, o_ik:o_ik + 72], 128)], axis=1)
    tm = min(512, T)
    pm = _proj(x2, norm_mix_g[0], w_main, precise=False, tm=min(1024, T), tn=768, name="proj_main")
    pp = _proj(x2, norm_mix_g[0], w_prec, precise=True, tm=tm, tn=512, name="proj_precise")

    head_id = jnp.arange(RW_WIDTH) // 64
    bd = (head_id[:, None] == head_id[None, :]).astype(BF16)
    mixes = [sm[:1024], sm[1024:2048], sm[2048:3072], sm[o_zw:o_zw + 128],
             jnp.pad(sm[o_zg:o_zg + 160], (0, 96))]
    wd = jnp.pad(w_decay_up[0], ((0, 64), (0, 0)))
    wi = jnp.pad(w_icl_up[0], ((64, 0), (0, 0)))
    wg = jnp.pad(w_gate_up[0], ((0, 96), (0, 0)))
    r, k4, v, kkn, b, lw, g, bonus = _rw_prep(pm, pp, mixes, w0[0], wd, a0[0], wi, wg, k_k[0], k_a[0],
                                              r_k[0].reshape(-1), bd, tm=min(256, T))
    nch = T // RW_CHUNK
    p_mat, q_mat, rp, y0 = _rw_chunk(r, k4, v, kkn, b, lw, G=min(8, nch))
    y = _rw_scan(p_mat, q_mat, rp, y0)

    wuk_t = jnp.transpose(w_uk[0], (0, 2, 1))
    wuk2 = jnp.stack([_block_diag2(wuk_t[2 * p], wuk_t[2 * p + 1]) for p in range(SA_HEADS // 2)]).astype(BF16)
    wuv2 = jnp.stack([_block_diag2(w_uv[0][2 * p], w_uv[0][2 * p + 1]) for p in range(SA_HEADS // 2)]).astype(BF16)
    c, ql = _dsa_prep(pm, kv_norm_g[0], wuk2, tm=min(512, T))
    o_dsa = _dsa(pp, *_dsa_operands(pp, c), ql, _rel_bias_tiles(rel_bias), wuv2, topk=topk)

    h1 = _merge(y, bonus, g, o_dsa, pm, x2, ln_x_w[0], ln_x_b[0], bd,
                w_branch_rwkv[0].astype(BF16), w_branch_dsa[0].astype(BF16), w_out[0].astype(BF16),
                tm=min(256, T))

    qp = _proj(h1, norm_ffn_g[0], w_peer_query[0], precise=True, tm=tm, tn=512, name="peer_query", lane_groups=True)
    keys = peer_sub_keys[0].reshape(2 * PEER_HEADS, PEER_KEYS, LANES)
    et, tau = _peer_select(qp, keys, tm=256)
    u = peer_u[0].astype(BF16)
    vt = jnp.transpose(peer_v[0].astype(BF16))
    out = _peer(h1, norm_ffn_g[0], norm_final_g, u, vt, et, tau, tm=min(512, T), e_tile=512)
    return out[None]
```

```python
import functools
import math

import numpy as np
import jax
import jax.numpy as jnp
from jax import lax
from jax.experimental import pallas as pl
from jax.experimental.pallas import tpu as pltpu

F32, BF16, I32 = jnp.float32, jnp.bfloat16, jnp.int32

D_MODEL = 2048
EPS = 1e-6
RW_WIDTH = 1024
RW_GN_EPS = 64e-5
RW_CHUNK = 128
RW_LOCKSTEP = 8
SA_HEADS = 16
SA_SCALE = 64 ** -0.5
KV_RANK = 256
IDX_HEADS = 8
IDX_SCALE = (8 * 64) ** -0.5
TOPK_MAX = 256
REL_BUCKETS = 32
REL_MAX_DIST = 1024
PEER_HEADS = 8
PEER_KEYS = 128
PEER_TOPK = 16
LANES = 128
SUBLANES = 8
NEG_BIG = -1e30
INT_MIN = -2 ** 31

PM_GATE_A, PM_GATE_B = 0, 1
PM_R, PM_K, PM_V, PM_Q = 4, 5, 6, 7
PM_CKV = 32
PP_ZG = 2
PP_ZWA = 6
PP_IK0, PP_IW0 = 896, 960

Q_BLK = 128
K_TILE = 256
N_BIAS_TILES = 7
R_TILES = 4
LOG2E = math.log2(math.e)


def _cp(sem, vmem_mb=48):
    return pltpu.CompilerParams(dimension_semantics=sem, vmem_limit_bytes=vmem_mb * 1024 * 1024)


def _bf(x):
    return x.astype(BF16)


def _split2(x):
    hi = x.astype(BF16)
    lo = (x - hi.astype(F32)).astype(BF16)
    return hi, lo


def _split2_glue(x):
    hi32 = lax.bitcast_convert_type(lax.bitcast_convert_type(x, jnp.uint32) & jnp.uint32(0xFFFF0000), F32)
    return hi32.astype(BF16), (x - hi32).astype(BF16)


def _mm(a, b):
    return jnp.dot(a, b, preferred_element_type=F32)


def _mm_nt(a, b):
    return lax.dot_general(a, b, (((1,), (1,)), ((), ())), preferred_element_type=F32)


def _mm_tn(a, b):
    return lax.dot_general(a, b, (((0,), (0,)), ((), ())), preferred_element_type=F32)


def _d3(a_s, b_s, mm=_mm):
    return mm(a_s[0], b_s[0]) + (mm(a_s[0], b_s[1]) + mm(a_s[1], b_s[0]))


def _dot3(a, b, mm=_mm):
    return _d3(_split2(a), _split2(b), mm)


def _proj_kernel(x_ref, g_ref, w_ref, o_ref, xn_ref, *, lane_groups):
    @pl.when(pl.program_id(1) == 0)
    def _():
        x = x_ref[...]
        xn_ref[...] = (x * lax.rsqrt(jnp.mean(x * x, axis=-1, keepdims=True) + EPS) * g_ref[...]).astype(BF16)

    res = _mm(xn_ref[...], w_ref[...])
    if lane_groups:
        for k in range(res.shape[1] // LANES):
            o_ref[k] = res[:, k * LANES:(k + 1) * LANES]
    else:
        o_ref[...] = res


def _proj(x, g, w, *, tm, tn, name, lane_groups=False):
    T, D = x.shape
    N = w.shape[1]
    if lane_groups:
        out_spec = pl.BlockSpec((tn // LANES, tm, LANES), lambda i, j: (j, i, 0))
        out_shape = jax.ShapeDtypeStruct((N // LANES, T, LANES), F32)
    else:
        out_spec = pl.BlockSpec((tm, tn), lambda i, j: (i, j))
        out_shape = jax.ShapeDtypeStruct((T, N), F32)
    return pl.pallas_call(
        functools.partial(_proj_kernel, lane_groups=lane_groups),
        grid=(T // tm, N // tn),
        in_specs=[pl.BlockSpec((tm, D), lambda i, j: (i, 0)),
                  pl.BlockSpec((1, D), lambda i, j: (0, 0)),
                  pl.BlockSpec((D, tn), lambda i, j: (0, j))],
        out_specs=out_spec,
        out_shape=out_shape,
        scratch_shapes=[pltpu.VMEM((tm, D), BF16)],
        compiler_params=_cp(("parallel", "arbitrary")),
        name=name,
    )(x, g.reshape(1, D), w)


def _rw_prep_kernel(r_ref, k_ref, v_ref, wa_ref, zg_ref, rp_ref, kp_ref, vp_ref, wap_ref, zgp_ref,
                    mr_ref, mk_ref, mv_ref, mwa_ref, mg_ref, w0_ref, wd_ref, a0_ref, wi_ref, wg_ref,
                    kk_ref, ka_ref, rk_ref, bd_ref,
                    r_out, k_out, v_out, kkn_out, b_out, lw_out, g_out, bonus_out):
    first = pl.program_id(0) == 0

    def shift(p_ref, prev_ref, mix_ref):
        p = p_ref[...]
        last = jnp.where(first, 0.0, prev_ref[SUBLANES - 1:SUBLANES, :])
        row = lax.broadcasted_iota(I32, p.shape, 0)
        prev = jnp.where(row == 0, last, pltpu.roll(p, 1, 0))
        return p + (prev - p) * mix_ref[...]

    r = shift(r_ref, rp_ref, mr_ref)
    k = shift(k_ref, kp_ref, mk_ref)
    v = shift(v_ref, vp_ref, mv_ref)
    zwa = shift(wa_ref, wap_ref, mwa_ref)
    zg = shift(zg_ref, zgp_ref, mg_ref)

    nd = -(w0_ref[...] + _dot3(jnp.tanh(zwa), wd_ref[...]))
    softplus = jnp.maximum(nd, 0.0) + jnp.log1p(jnp.exp(-jnp.abs(nd)))
    lw = -jnp.exp(-softplus - 0.5)
    a = jax.nn.sigmoid(a0_ref[...] + _dot3(zwa, wi_ref[...]))
    g = _dot3(jax.nn.sigmoid(zg), wg_ref[...])

    bd = bd_ref[...]
    wide = bd.shape[0]

    def head_sum(x):
        hi, lo = _split2(x)
        parts = [_mm(hi[:, p * wide:(p + 1) * wide], bd) + _mm(lo[:, p * wide:(p + 1) * wide], bd)
                 for p in range(RW_WIDTH // wide)]
        return jnp.concatenate(parts, axis=1)

    kk = k * kk_ref[...]
    kkn = kk / jnp.maximum(jnp.sqrt(head_sum(kk * kk)), 1e-12)
    k4 = k * (1.0 + (a - 1.0) * ka_ref[...])
    r_out[...] = r
    k_out[...] = k4
    v_out[...] = v
    kkn_out[...] = kkn
    b_out[...] = kkn * a
    lw_out[...] = lw
    g_out[...] = g
    bonus_out[...] = head_sum(r * k4 * rk_ref[...]) * v


def _rw_prep(pm, pp, mixes, w0, wd, a0, wi, wg, k_k, k_a, r_k, bd, *, tm):
    T = pm.shape[0]
    W = RW_WIDTH
    pb = tm // SUBLANES

    def cur(width, col):
        return pl.BlockSpec((tm, width), lambda i: (i, col))

    def prev(width, col):
        return pl.BlockSpec((SUBLANES, width), lambda i: (jnp.maximum(i * pb - 1, 0), col))

    def full(shape):
        return pl.BlockSpec(shape, lambda i: (0,) * len(shape))

    row = lambda a: a.reshape(1, -1)
    out_sd = jax.ShapeDtypeStruct((T, W), F32)
    return pl.pallas_call(
        _rw_prep_kernel,
        grid=(T // tm,),
        in_specs=[cur(W, PM_R), cur(W, PM_K), cur(W, PM_V), cur(128, PP_ZWA), cur(256, PP_ZG),
                  prev(W, PM_R), prev(W, PM_K), prev(W, PM_V), prev(128, PP_ZWA), prev(256, PP_ZG),
                  full((1, W)), full((1, W)), full((1, W)), full((1, 128)), full((1, 256)),
                  full((1, W)), full((128, W)), full((1, W)), full((128, W)), full((256, W)),
                  full((1, W)), full((1, W)), full((1, W)), full((2 * LANES, 2 * LANES))],
        out_specs=[pl.BlockSpec((tm, W), lambda i: (i, 0))] * 8,
        out_shape=[out_sd] * 8,
        compiler_params=_cp(("parallel",)),
        name="rw_prep",
    )(pm, pm, pm, pp, pp, pm, pm, pm, pp, pp,
      *[row(m) for m in mixes], row(w0), wd, row(a0), wi, wg, row(k_k), row(k_a), row(r_k), bd[:2 * LANES, :2 * LANES])


def _rw_chunk_kernel(r_ref, k_ref, v_ref, kkn_ref, b_ref, lw_ref, p_out, q_out, rp_out, y0_out, *, G):
    C = RW_CHUNK
    row = lax.broadcasted_iota(I32, (C, C), 0)
    col = lax.broadcasted_iota(I32, (C, C), 1)
    incl = row >= col
    strict = row > col
    eye = row == col
    same_head = (row // 64) == (col // 64)
    tri = jnp.where(incl, 1.0, 0.0).astype(BF16)
    lane = lax.broadcasted_iota(I32, (C, LANES), 1)
    eye_f = jnp.where(eye, 1.0, 0.0)

    head_lanes = (lane < 64, lane >= 64)
    U = RW_LOCKSTEP

    def body(cu, carry):
        ch = [cu * U + u for u in range(U)]
        sls = [pl.ds(pl.multiple_of(c * C, C), C) for c in ch]
        pre = []
        for sl in sls:
            r, k4, v, kkn, b, lw = (ref[sl, :] for ref in (r_ref, k_ref, v_ref, kkn_ref, b_ref, lw_ref))
            h1 = lw.astype(BF16)
            r1 = lw - h1.astype(F32)
            h2 = r1.astype(BF16)
            h3 = (r1 - h2.astype(F32)).astype(BF16)
            L = _mm(tri, h1) + (_mm(tri, h2) + _mm(tri, h3))
            Ltot = L[C - 1:C, :]
            Lr = L - L[C // 2 - 1:C // 2, :]
            eneg = jnp.exp(-Lr)
            tail = jnp.exp(Ltot - L)
            pre.append(dict(
                rhat=r * jnp.exp(Lr), ahat=-kkn * jnp.exp(Lr - lw), kt=_bf(k4 * eneg), bt=_bf(b * eneg),
                rabs=r * jnp.exp(L), aabs=-kkn * jnp.exp(L - lw), kbar=_bf(k4 * tail), bbar=_bf(b * tail),
                gam=jnp.exp(Ltot), vb=_bf(v)))
        inst = [(u, hh) for u in range(U) for hh in range(2)]
        am = [_bf(jnp.where(head_lanes[hh], pre[u]["ahat"], 0.0)) for u, hh in inst]
        rm = [_bf(jnp.where(head_lanes[hh], pre[u]["rhat"], 0.0)) for u, hh in inst]
        a_ab = [jnp.where(strict, _mm_nt(am[k], pre[u]["bt"]), 0.0) for k, (u, hh) in enumerate(inst)]
        a_ak = [_bf(jnp.where(strict, _mm_nt(am[k], pre[u]["kt"]), 0.0)) for k, (u, hh) in enumerate(inst)]
        m_rb = [_bf(jnp.where(incl, _mm_nt(rm[k], pre[u]["bt"]), 0.0)) for k, (u, hh) in enumerate(inst)]
        m_rk = [_bf(jnp.where(incl, _mm_nt(rm[k], pre[u]["kt"]), 0.0)) for k, (u, hh) in enumerate(inst)]
        n = a_ab
        x = [eye_f + t for t in n]
        for _ in range(6):
            nb = [_bf(t) for t in n]
            n = [_mm(t, t) for t in nb]
            x = [xi + _mm(_bf(xi), _bf(ni)) for xi, ni in zip(x, n)]
        xb = [_bf(t) for t in x]
        akv = [_bf(_mm(a_ak[k], pre[u]["vb"])) for k, (u, hh) in enumerate(inst)]
        ah_h = [_mm(xb[k], _bf(jnp.where(head_lanes[hh], pre[u]["aabs"], 0.0))) for k, (u, hh) in enumerate(inst)]
        u0_h = [_mm(xb[k], akv[k]) for k in range(len(inst))]
        y0_h = [_mm(m_rk[k], pre[u]["vb"]) + _mm(m_rb[k], _bf(u0_h[k])) for k, (u, hh) in enumerate(inst)]
        rp_h = [jnp.where(head_lanes[hh], pre[u]["rabs"], 0.0) + _mm(m_rb[k], _bf(ah_h[k]))
                for k, (u, hh) in enumerate(inst)]
        for u in range(U):
            lo, hi = 2 * u, 2 * u + 1
            ah = _bf(ah_h[lo] + ah_h[hi])
            u0 = _bf(jnp.where(head_lanes[0], u0_h[lo], u0_h[hi]))
            pu = pre[u]
            p_out[0, ch[u]] = jnp.where(eye, pu["gam"], 0.0) + jnp.where(same_head, _mm_tn(pu["bbar"], ah), 0.0)
            q_out[0, ch[u]] = jnp.where(same_head, _mm_tn(pu["kbar"], pu["vb"]) + _mm_tn(pu["bbar"], u0), 0.0)
            rp_out[sls[u], :] = rp_h[lo] + rp_h[hi]
            y0_out[sls[u], :] = jnp.where(head_lanes[0], y0_h[lo], y0_h[hi])
        return carry

    lax.fori_loop(0, G // U, body, 0)


def _rw_chunk(r, k4, v, kkn, b, lw, *, G):
    T = r.shape[0]
    C = RW_CHUNK
    nch = T // C
    rows = G * C
    cur = pl.BlockSpec((rows, LANES), lambda p, g: (g, p))
    mat = pl.BlockSpec((1, G, C, C), lambda p, g: (p, g, 0, 0))
    return pl.pallas_call(
        functools.partial(_rw_chunk_kernel, G=G),
        grid=(RW_WIDTH // LANES, nch // G),
        in_specs=[cur] * 6,
        out_specs=[mat, mat, cur, cur],
        out_shape=[jax.ShapeDtypeStruct((RW_WIDTH // LANES, nch, C, C), F32)] * 2
        + [jax.ShapeDtypeStruct((T, RW_WIDTH), F32)] * 2,
        compiler_params=_cp(("parallel", "parallel")),
        name="rw_chunk",
    )(r, k4, v, kkn, b, lw)


def _rw_scan_kernel(p_ref, q_ref, rp_ref, y0_ref, y_out, s_scr):
    @pl.when(pl.program_id(0) == 0)
    def _():
        s_scr[...] = jnp.zeros_like(s_scr)

    for p in range(RW_WIDTH // LANES):
        ls = slice(p * LANES, (p + 1) * LANES)
        s0 = _bf(s_scr[p])
        y_out[:, ls] = _mm(_bf(rp_ref[:, ls]), s0) + y0_ref[:, ls]
        s_scr[p] = _mm(_bf(p_ref[p, 0]), s0) + q_ref[p, 0]


def _rw_scan(p_mat, q_mat, rp, y0):
    T = rp.shape[0]
    C = RW_CHUNK
    npair = RW_WIDTH // LANES
    mat = pl.BlockSpec((npair, 1, C, C), lambda c: (0, c, 0, 0))
    cur = pl.BlockSpec((C, RW_WIDTH), lambda c: (c, 0))
    return pl.pallas_call(
        _rw_scan_kernel,
        grid=(T // C,),
        in_specs=[mat, mat, cur, cur],
        out_specs=cur,
        out_shape=jax.ShapeDtypeStruct((T, RW_WIDTH), F32),
        scratch_shapes=[pltpu.VMEM((npair, C, C), F32)],
        compiler_params=_cp(("arbitrary",)),
        name="rw_scan",
    )(p_mat, q_mat, rp, y0)


def _dsa_prep_kernel(q_ref, ckv_ref, g_ref, wuk_ref, c_out, ql_out):
    ckv = ckv_ref[...]
    c = ckv * lax.rsqrt(jnp.mean(ckv * ckv, axis=-1, keepdims=True) + EPS) * g_ref[...]
    c_out[...] = c.astype(BF16)
    for p in range(SA_HEADS // 2):
        qp = q_ref[:, p * LANES:(p + 1) * LANES].astype(BF16)
        ql = _mm(qp, wuk_ref[p]) * (SA_SCALE * LOG2E)
        ql_out[2 * p] = ql[:, :KV_RANK].astype(BF16)
        ql_out[2 * p + 1] = ql[:, KV_RANK:].astype(BF16)


def _dsa_prep(pm, kv_norm_g, wuk2, *, tm):
    T = pm.shape[0]
    return pl.pallas_call(
        _dsa_prep_kernel,
        grid=(T // tm,),
        in_specs=[pl.BlockSpec((tm, 1024), lambda i: (i, PM_Q)),
                  pl.BlockSpec((tm, KV_RANK), lambda i: (i, PM_CKV)),
                  pl.BlockSpec((1, KV_RANK), lambda i: (0, 0)),
                  pl.BlockSpec((SA_HEADS // 2, LANES, 2 * KV_RANK), lambda i: (0, 0, 0))],
        out_specs=[pl.BlockSpec((tm, KV_RANK), lambda i: (i, 0)),
                   pl.BlockSpec((SA_HEADS, tm, KV_RANK), lambda i: (0, i, 0))],
        out_shape=[jax.ShapeDtypeStruct((T, KV_RANK), BF16),
                   jax.ShapeDtypeStruct((SA_HEADS, T, KV_RANK), BF16)],
        compiler_params=_cp(("parallel",)),
        name="dsa_prep",
    )(pm, pm, kv_norm_g.reshape(1, KV_RANK), wuk2)


def _bucket_bounds():
    nb = REL_BUCKETS // 2
    max_exact = nb // 2
    n = np.arange(1, 2 * REL_MAX_DIST, dtype=np.int64)
    large = max_exact + (np.log(n.astype(np.float32) / max_exact)
                         / math.log(REL_MAX_DIST / max_exact) * (nb - max_exact)).astype(np.int32)
    large = np.minimum(large, nb - 1)
    bounds = [int(n[np.argmax(large >= b)]) for b in range(max_exact + 1, nb)]
    assert all(b2 > b1 for b1, b2 in zip(bounds, bounds[1:]))
    return bounds


_BUCKET_BOUNDS = _bucket_bounds()


def _rel_bias_kernel(rb_ref, bt_out):
    o = pl.program_id(0)
    sl = lax.broadcasted_iota(I32, (K_TILE, Q_BLK), 0)
    tl = lax.broadcasted_iota(I32, (K_TILE, Q_BLK), 1)
    rel = sl - tl - o * Q_BLK
    nb = REL_BUCKETS // 2
    max_exact = nb // 2
    n = jnp.abs(rel)
    large = jnp.full(rel.shape, max_exact, I32)
    for bound in _BUCKET_BOUNDS:
        large = large + jnp.where(n >= bound, 1, 0)
    bucket = jnp.where(rel > 0, nb, 0) + jnp.where(n < max_exact, n, large)
    for h in range(SA_HEADS):
        acc = jnp.zeros(rel.shape, F32)
        for bkt in range(REL_BUCKETS):
            acc = jnp.where(bucket == bkt, rb_ref[bkt, h], acc)
        bt_out[0, h] = (acc - rb_ref[nb - 1, h]) * LOG2E


def _rel_bias_tiles(rel_bias):
    return pl.pallas_call(
        _rel_bias_kernel,
        grid=(N_BIAS_TILES,),
        in_specs=[pl.BlockSpec(memory_space=pltpu.SMEM)],
        out_specs=pl.BlockSpec((1, SA_HEADS, K_TILE, Q_BLK), lambda o: (o, 0, 0, 0)),
        out_shape=jax.ShapeDtypeStruct((N_BIAS_TILES, SA_HEADS, K_TILE, Q_BLK), F32),
        compiler_params=_cp(("parallel",)),
        name="rel_bias",
    )(rel_bias)


def _dsa_kernel(iq_ref, iwt_ref, ql_ref, ik_ref, c_ref, ct_ref, bt_ref, wuv_ref, o_out,
                key_scr, iq_scr, m_scr, alpha_scr, p_scr, acc_scr, *, topk):
    i = pl.program_id(0)
    t0 = i * Q_BLK
    n_tiles = i // 2 + 1
    H = SA_HEADS
    lane_q = lax.broadcasted_iota(I32, (Q_BLK, LANES), 1)
    first = lane_q < 64

    for p in range(IDX_HEADS // 2):
        pair = iq_ref[:, p * LANES:(p + 1) * LANES]
        swap = pltpu.roll(pair, 64, 1)
        rows = []
        for own, other in ((pair, swap), (swap, pair)):
            o_hi, o_lo = _split2(own)
            t_hi = other.astype(BF16)
            rows.append(jnp.concatenate([jnp.where(first, o_hi, t_hi), jnp.where(first, o_lo, jnp.zeros_like(o_lo))], axis=1))
        iq_scr[p] = jnp.concatenate(rows, axis=0)
    w_rows = [iwt_ref[h:h + 1, :] * IDX_SCALE for h in range(IDX_HEADS)]

    key_pos = lax.broadcasted_iota(I32, (K_TILE, Q_BLK), 0)
    q_lane = lax.broadcasted_iota(I32, (K_TILE, Q_BLK), 1)
    limit = t0 + (q_lane // 64 + 1) * 64

    def score_tiles(jj, carry):
        starts = [pl.multiple_of((2 * jj + u) * K_TILE, K_TILE) for u in range(2)]
        dots = [[_mm_nt(ik_ref[pl.ds(s0, K_TILE), :], iq_scr[p]) for p in range(IDX_HEADS // 2)] for s0 in starts]
        for s0, tile_dots in zip(starts, dots):
            sc = jnp.zeros((K_TILE, Q_BLK), F32)
            for p, d in enumerate(tile_dots):
                sc = (sc + jnp.maximum(d[:, :Q_BLK], 0.0) * w_rows[2 * p]
                      + jnp.maximum(d[:, Q_BLK:], 0.0) * w_rows[2 * p + 1])
            sc = sc + 0.0
            key_scr[pl.ds(s0, K_TILE), :] = jnp.where(s0 + key_pos < limit, sc, -jnp.inf)
        return carry

    lax.fori_loop(0, (n_tiles + 1) // 2, score_tiles, 0)

    n_steps = (n_tiles + R_TILES - 1) // R_TILES

    def pad_tile(j, carry):
        key_scr[pl.ds(pl.multiple_of(j * K_TILE, K_TILE), K_TILE), :] = jnp.full((K_TILE, Q_BLK), -jnp.inf, F32)
        return carry

    lax.fori_loop(n_tiles, n_steps * R_TILES, pad_tile, 0)

    def count(pred):
        def step(js, acc):
            s0 = pl.multiple_of(js * (R_TILES * K_TILE), R_TILES * K_TILE)
            for sub in range(R_TILES * K_TILE // 64):
                hit = jnp.where(pred(key_scr[pl.ds(s0 + sub * 64, 64), :]), 1.0, 0.0)
                acc = acc + jnp.sum(hit.reshape(64 // SUBLANES, SUBLANES, Q_BLK), axis=0)
            return acc
        acc = lax.fori_loop(0, n_steps, step, jnp.zeros((SUBLANES, Q_BLK), F32))
        return jnp.sum(acc, axis=0, keepdims=True)

    def as_float(ordered):
        return pltpu.bitcast(ordered ^ ((ordered >> 31) & 0x7FFFFFFF), F32)

    def bit_step(it, cur):
        bit = lax.shift_left(jnp.int32(1), 31 - it)
        cand = as_float((cur | bit) ^ INT_MIN)
        cnt = count(lambda kt: kt >= cand)
        return jnp.where(cnt >= topk, cur | bit, cur)

    cur = lax.fori_loop(0, 32, bit_step, jnp.zeros((1, Q_BLK), I32))
    take_all = limit[:1, :] <= topk
    theta = jnp.where(take_all, -jnp.inf, as_float(cur ^ INT_MIN))
    need = jnp.where(take_all, 0.0, topk - count(lambda kt: kt > theta))

    lr = lax.broadcasted_iota(I32, (K_TILE, K_TILE), 0)
    lc = lax.broadcasted_iota(I32, (K_TILE, K_TILE), 1)
    lower = jnp.where(lr >= lc, 1.0, 0.0).astype(BF16)

    def mask_tiles(jj, carry):
        starts = [pl.multiple_of((2 * jj + u) * K_TILE, K_TILE) for u in range(2)]
        kts = [key_scr[pl.ds(s0, K_TILE), :] for s0 in starts]
        eqs = [jnp.where(kt == theta, 1.0, 0.0) for kt in kts]
        within = [_mm(lower, e.astype(BF16)) for e in eqs]
        for s0, kt, e, w in zip(starts, kts, eqs, within):
            tie = jnp.where(carry + w <= need, 0.0, NEG_BIG)
            addm = jnp.where(kt > theta, 0.0, jnp.where(kt == theta, tie, NEG_BIG))
            key_scr[pl.ds(s0, K_TILE), :] = addm
            carry = carry + jnp.sum(e, axis=0, keepdims=True)
        return carry

    lax.fori_loop(0, (n_tiles + 1) // 2, mask_tiles, jnp.zeros((1, Q_BLK), F32))

    m_scr[...] = jnp.full(m_scr.shape, NEG_BIG, F32)
    acc_scr[...] = jnp.zeros(acc_scr.shape, F32)
    p_scr[...] = jnp.zeros(p_scr.shape, BF16)
    alpha_scr[...] = jnp.ones(alpha_scr.shape, F32)

    def apply_pending(ct_t, ls):
        acc_scr[:, ls] = alpha_scr[:, ls] * acc_scr[:, ls] + _mm(ct_t, p_scr[:, ls])

    def attn_tile(j, carry, *, near):
        s0 = pl.multiple_of(j * K_TILE, K_TILE)
        c_t = c_ref[pl.ds(s0, K_TILE), :]
        ct_prev = ct_ref[:, pl.ds(pl.multiple_of(jnp.maximum(j - 1, 0) * K_TILE, K_TILE), K_TILE)]
        addm = key_scr[pl.ds(s0, K_TILE), :]
        addm2 = jnp.concatenate([addm, addm], axis=1)

        def pair_logits(p):
            return _mm_nt(c_t, ql_ref[2 * p:2 * p + 2].reshape(2 * Q_BLK, KV_RANK))

        lg2_next = pair_logits(0)
        for p in range(H // 2):
            lg2 = lg2_next
            if p + 1 < H // 2:
                lg2_next = pair_logits(p + 1)
            apply_pending(ct_prev, slice(2 * p * Q_BLK, (2 * p + 2) * Q_BLK))
            for hh in range(2):
                h = 2 * p + hh
                ls = slice(h * Q_BLK, (h + 1) * Q_BLK)
                lg = lg2[:, hh * Q_BLK:(hh + 1) * Q_BLK] + addm
                if near:
                    lg = lg + bt_ref[i - 2 * j, h]
                m_prev = m_scr[:, ls]
                m_new = jnp.maximum(m_prev, jnp.max(lg, axis=0, keepdims=True))
                m_scr[:, ls] = m_new
                alpha_scr[:, ls] = jnp.exp2(m_prev - m_new)
                p_scr[:, ls] = jnp.exp2(lg - m_new).astype(BF16)
        return carry

    n_far = jnp.maximum(i - N_BIAS_TILES + 2, 0) // 2
    lax.fori_loop(0, n_far, functools.partial(attn_tile, near=False), 0)
    lax.fori_loop(n_far, n_tiles, functools.partial(attn_tile, near=True), 0)
    ct_last = ct_ref[:, pl.ds(pl.multiple_of((n_tiles - 1) * K_TILE, K_TILE), K_TILE)]
    for p in range(H // 2):
        apply_pending(ct_last, slice(2 * p * Q_BLK, (2 * p + 2) * Q_BLK))

    o_lat = (acc_scr[:KV_RANK, :] / acc_scr[KV_RANK:KV_RANK + 1, :]).T.astype(BF16)
    for p in range(H // 2):
        pair = jnp.concatenate([o_lat[2 * p * Q_BLK:(2 * p + 1) * Q_BLK], o_lat[(2 * p + 1) * Q_BLK:(2 * p + 2) * Q_BLK]],
                               axis=1)
        o_out[:, p * LANES:(p + 1) * LANES] = _mm(pair, wuv_ref[p])


ONES_ROWS = 16


def _dsa_operands(pp, c):
    ik = pp[:, PP_IK0:PP_IK0 + 64]
    ik_hi, ik_lo = _split2_glue(ik)
    ik_packed = jnp.concatenate([ik_hi, ik_lo, ik_hi, jnp.zeros_like(ik_hi)], axis=1)
    iwt = jnp.transpose(pp[:, PP_IW0:PP_IW0 + IDX_HEADS])
    ct_ext = jnp.concatenate([jnp.transpose(c), jnp.ones((ONES_ROWS, c.shape[0]), c.dtype)], axis=0)
    return iwt, ik_packed, c, ct_ext


def _dsa(pp, iwt, ik_packed, c, ct_ext, ql, bt, wuv2, *, topk):
    T = pp.shape[0]
    H = SA_HEADS
    assert T % (R_TILES * K_TILE) == 0 and N_BIAS_TILES * Q_BLK - (K_TILE - 1) >= _BUCKET_BOUNDS[-1]
    whole = pl.BlockSpec(memory_space=pltpu.VMEM)
    return pl.pallas_call(
        functools.partial(_dsa_kernel, topk=topk),
        grid=(T // Q_BLK,),
        in_specs=[pl.BlockSpec((Q_BLK, 512), lambda i: (i, 0)),
                  pl.BlockSpec((IDX_HEADS, Q_BLK), lambda i: (0, i)),
                  pl.BlockSpec((H, Q_BLK, KV_RANK), lambda i: (0, i, 0)),
                  whole, whole, whole, whole, whole],
        out_specs=pl.BlockSpec((Q_BLK, 1024), lambda i: (i, 0)),
        out_shape=jax.ShapeDtypeStruct((T, 1024), F32),
        scratch_shapes=[pltpu.VMEM((T, Q_BLK), F32),
                        pltpu.VMEM((IDX_HEADS // 2, 2 * Q_BLK, 2 * LANES), BF16),
                        pltpu.VMEM((1, H * Q_BLK), F32),
                        pltpu.VMEM((1, H * Q_BLK), F32),
                        pltpu.VMEM((K_TILE, H * Q_BLK), BF16),
                        pltpu.VMEM((KV_RANK + ONES_ROWS, H * Q_BLK), F32)],
        compiler_params=_cp(("parallel",), 56),
        name="dsa",
    )(pp, iwt, ql, ik_packed, c, ct_ext, bt, wuv2)


def _merge_kernel(y_ref, bonus_ref, g_ref, o_ref, ga_ref, gb_ref, x_ref, lnw_ref, lnb_ref, bd_ref,
                  wa_ref, wb_ref, wout_ref, h_out):
    bd = bd_ref[...]
    wide = bd.shape[0]

    def head_mean(x):
        hi, lo = _split2(x)
        parts = [_mm(hi[:, p * wide:(p + 1) * wide], bd) + _mm(lo[:, p * wide:(p + 1) * wide], bd)
                 for p in range(RW_WIDTH // wide)]
        return jnp.concatenate(parts, axis=1) * (1.0 / 64)

    y = y_ref[...]
    yc = y - head_mean(y)
    yn = yc * lax.rsqrt(head_mean(yc * yc) + RW_GN_EPS) * lnw_ref[...] + lnb_ref[...]
    ya = (yn + bonus_ref[...]) * g_ref[...]
    pa = _mm(ya.astype(BF16), wa_ref[...])
    pb = _mm(o_ref[...].astype(BF16), wb_ref[...])
    m = jax.nn.sigmoid(ga_ref[...]) * pa + jax.nn.sigmoid(gb_ref[...]) * pb
    h_out[...] = x_ref[...] + _mm(m.astype(BF16), wout_ref[...])


def _merge(y, bonus, g, o, pm, x, ln_w, ln_b, bd, wa, wb, wout, *, tm):
    T, D = x.shape
    W = RW_WIDTH
    row = lambda a: a.reshape(1, -1)
    act = pl.BlockSpec((tm, W), lambda i: (i, 0))
    full = lambda shape: pl.BlockSpec(shape, lambda i: (0,) * len(shape))
    whole = pl.BlockSpec(memory_space=pltpu.VMEM)
    return pl.pallas_call(
        _merge_kernel,
        grid=(T // tm,),
        in_specs=[act, act, act, act,
                  pl.BlockSpec((tm, D), lambda i: (i, PM_GATE_A)),
                  pl.BlockSpec((tm, D), lambda i: (i, PM_GATE_B)),
                  pl.BlockSpec((tm, D), lambda i: (i, 0)),
                  full((1, W)), full((1, W)), whole, whole, whole, whole],
        out_specs=pl.BlockSpec((tm, D), lambda i: (i, 0)),
        out_shape=jax.ShapeDtypeStruct((T, D), F32),
        compiler_params=_cp(("parallel",), 56),
        name="merge",
    )(y, bonus, g, o, pm, pm, x, row(ln_w), row(ln_b), bd[:2 * LANES, :2 * LANES], wa, wb, wout)


def _sort16_pairs():
    n, pairs, p = 16, [], 1
    while p < n:
        k = p
        while k >= 1:
            for j in range(k % p, n - k, 2 * k):
                for i in range(min(k, n - j - k)):
                    if (i + j) // (2 * p) == (i + j + k) // (2 * p):
                        pairs.append((i + j, i + j + k))
            k //= 2
        p *= 2
    return pairs


_SORT16 = _sort16_pairs()
PEER_MARGIN = 1.0 - 2.0 ** -20


def _exchange(a, i, j):
    a[i], a[j] = jnp.maximum(a[i], a[j]), jnp.minimum(a[i], a[j])


def _bitonic_sort16(a):
    for dist in (8, 4, 2, 1):
        for i in range(PEER_TOPK):
            if not i & dist:
                _exchange(a, i, i + dist)
    return a


def _top16_merge(x, y):
    return _bitonic_sort16([jnp.maximum(x[i], y[PEER_TOPK - 1 - i]) for i in range(PEER_TOPK)])


def _peer_select_kernel(qp_ref, keys_ref, et_out, tau_out):
    tm = qp_ref.shape[1]
    sub = lax.broadcasted_iota(I32, (SUBLANES, tm), 0)
    zero = jnp.zeros((SUBLANES, tm), F32)

    def per_head(h, packed):
        packed = list(packed)
        for c in range(2):
            hc = 2 * h + c
            s_t = _dot3(keys_ref[hc], qp_ref[hc], _mm_nt)
            e = jnp.exp(s_t - jnp.max(s_t, axis=0, keepdims=True))
            et_out[hc] = e
            a = [e[g * SUBLANES:(g + 1) * SUBLANES, :] for g in range(PEER_KEYS // SUBLANES)]
            for i, j in _SORT16:
                _exchange(a, i, j)
            for shift in (4, 2, 1):
                a = _top16_merge(a, [pltpu.roll(t, shift, 0) for t in a])
            for i in range(PEER_TOPK):
                packed[c * PEER_TOPK + i] = jnp.where(sub == h, a[i], packed[c * PEER_TOPK + i])
        return tuple(packed)

    packed = lax.fori_loop(0, PEER_HEADS, per_head, (zero,) * (2 * PEER_TOPK))
    top_a, top_b = packed[:PEER_TOPK], packed[PEER_TOPK:]

    def run(i, n):
        return [top_a[i] * top_b[j] for j in range(n)]

    g1 = run(0, 16)
    g2 = _bitonic_sort16(run(1, 8) + [top_a[i] * top_b[0] for i in range(PEER_TOPK - 1, 7, -1)])
    g3 = run(2, 5) + run(3, 4) + run(4, 3) + run(5, 2) + run(6, 2)
    for i, j in _SORT16:
        _exchange(g3, i, j)
    top = _top16_merge(_top16_merge(_top16_merge(g1, g2), g3), run(7, 2) + [zero] * 14)
    z = top[0]
    for t in top[1:]:
        z = z + t
    zinv = 1.0 / z
    tau_out[...] = top[PEER_TOPK - 1] * PEER_MARGIN * zinv
    for h in range(PEER_HEADS):
        et_out[2 * h] = et_out[2 * h] * zinv[h:h + 1, :]


def _peer_select(qp, keys, *, tm):
    T = qp.shape[1]
    return pl.pallas_call(
        _peer_select_kernel,
        grid=(T // tm,),
        in_specs=[pl.BlockSpec((2 * PEER_HEADS, tm, LANES), lambda i: (0, i, 0)),
                  pl.BlockSpec((2 * PEER_HEADS, PEER_KEYS, LANES), lambda i: (0, 0, 0))],
        out_specs=[pl.BlockSpec((2 * PEER_HEADS, PEER_KEYS, tm), lambda i: (0, 0, i)),
                   pl.BlockSpec((SUBLANES, tm), lambda i: (0, i))],
        out_shape=[jax.ShapeDtypeStruct((2 * PEER_HEADS, PEER_KEYS, T), F32),
                   jax.ShapeDtypeStruct((PEER_HEADS, T), F32)],
        compiler_params=_cp(("parallel",)),
        name="peer_select",
    )(qp, keys)


def _peer_kernel(h_ref, gf_ref, gout_ref, u_ref, vt_ref, et_ref, tau_ref, out_ref, hn_scr, acc_scr, ga0_scr, ga1_scr,
                 *, e_tile):
    j = pl.program_id(1)
    n_tiles = pl.num_programs(1) - 1
    tm = h_ref.shape[0]
    per_tile = e_tile // PEER_KEYS

    @pl.when(j == 0)
    def _():
        h = h_ref[...]
        hn = h * lax.rsqrt(jnp.mean(h * h, axis=-1, keepdims=True) + EPS) * gf_ref[...]
        hn_scr[...] = hn.astype(BF16)
        acc_scr[...] = jnp.zeros_like(acc_scr)
        ga1_scr[...] = jnp.zeros_like(ga1_scr)

    d_model = vt_ref.shape[0]
    half = e_tile // 2

    def pre_act(s):
        return _mm_nt(u_ref[s * half:(s + 1) * half, :], hn_scr[...])

    def apply_values(q, ga_read):
        rows = slice(q * (d_model // 4), (q + 1) * (d_model // 4))
        acc_scr[rows, :] += _mm(vt_ref[rows, :], ga_read[...])

    RC = 32

    def gated(s, pre, ga_write):
        for ai in range(per_tile // 2):
            a = jnp.minimum(j, n_tiles - 1) * per_tile + s * (per_tile // 2) + ai
            e1 = [et_ref[2 * h, pl.ds(a, 1), :] for h in range(PEER_HEADS)]
            for rc in range(PEER_KEYS // RC):
                x = pre[ai * PEER_KEYS + rc * RC:ai * PEER_KEYS + (rc + 1) * RC, :]
                act = 0.5 * x * (1.0 + lax.erf(x * (2.0 ** -0.5)))
                gate = jnp.zeros((RC, tm), F32)
                for h in range(PEER_HEADS):
                    w = et_ref[2 * h + 1, rc * RC:(rc + 1) * RC, :] * e1[h]
                    gate = gate + jnp.where(w >= tau_ref[h:h + 1, :], w, 0.0)
                r0 = s * half + ai * PEER_KEYS + rc * RC
                ga_write[r0:r0 + RC, :] = (gate * act).astype(BF16)

    def step(ga_write, ga_read):
        pre0 = pre_act(0)
        apply_values(0, ga_read)
        pre1 = pre_act(1)
        gated(0, pre0, ga_write)
        apply_values(1, ga_read)
        apply_values(2, ga_read)
        apply_values(3, ga_read)
        gated(1, pre1, ga_write)

    @pl.when(j % 2 == 0)
    def _():
        step(ga0_scr, ga1_scr)

    @pl.when(j % 2 == 1)
    def _():
        step(ga1_scr, ga0_scr)

    @pl.when(j == n_tiles)
    def _():
        h3 = h_ref[...] + acc_scr[...].T
        out_ref[...] = h3 * lax.rsqrt(jnp.mean(h3 * h3, axis=-1, keepdims=True) + EPS) * gout_ref[...]


def _peer(h1, g_ffn, g_out, u, vt, et, tau, *, tm, e_tile):
    T, D = h1.shape
    E = u.shape[0]
    n_tiles = E // e_tile
    assert n_tiles % 2 == 0
    return pl.pallas_call(
        functools.partial(_peer_kernel, e_tile=e_tile),
        grid=(T // tm, n_tiles + 1),
        in_specs=[pl.BlockSpec((tm, D), lambda i, j: (i, 0)),
                  pl.BlockSpec((1, D), lambda i, j: (0, 0)),
                  pl.BlockSpec((1, D), lambda i, j: (0, 0)),
                  pl.BlockSpec((e_tile, D), lambda i, j: (jnp.minimum(j, n_tiles - 1), 0)),
                  pl.BlockSpec((D, e_tile), lambda i, j: (0, jnp.maximum(j - 1, 0))),
                  pl.BlockSpec((2 * PEER_HEADS, PEER_KEYS, tm), lambda i, j: (0, 0, i)),
                  pl.BlockSpec((PEER_HEADS, tm), lambda i, j: (0, i))],
        out_specs=pl.BlockSpec((tm, D), lambda i, j: (i, 0)),
        out_shape=jax.ShapeDtypeStruct((T, D), F32),
        scratch_shapes=[pltpu.VMEM((tm, D), BF16), pltpu.VMEM((D, tm), F32),
                        pltpu.VMEM((e_tile, tm), BF16), pltpu.VMEM((e_tile, tm), BF16)],
        compiler_params=_cp(("parallel", "arbitrary"), 56),
        name="peer",
    )(h1, g_ffn.reshape(1, D), g_out.reshape(1, D), u, vt, et, tau)


def _pad_cols(a, n):
    return jnp.pad(a, ((0, 0), (0, n - a.shape[1])))


def _block_diag2(a, b):
    za = jnp.zeros((a.shape[0], b.shape[1]), a.dtype)
    zb = jnp.zeros((b.shape[0], a.shape[1]), a.dtype)
    return jnp.concatenate([jnp.concatenate([a, za], axis=1), jnp.concatenate([zb, b], axis=1)], axis=0)


def kernel(x, norm_mix_g, w_in, shift_mix, w0, w_decay_up, a0, w_icl_up, w_gate_up, k_k, k_a, r_k, ln_x_w, ln_x_b,
           kv_norm_g, w_uk, w_uv, rel_bias, w_branch_rwkv, w_branch_dsa, w_out, norm_ffn_g, w_peer_query,
           peer_sub_keys, peer_u, peer_v, norm_final_g):
    B, T, D = x.shape
    assert B == 1 and D == D_MODEL and T % (2 * K_TILE) == 0 and norm_mix_g.shape[0] == 1
    topk = min(TOPK_MAX, T // 4)
    x2 = x[0]
    w = w_in[0]
    sm = shift_mix[0]
    o_zw, o_za, o_zg, o_q, o_ckv, o_iq, o_ik, o_iw, o_g = 3072, 3136, 3200, 3360, 4384, 4640, 5152, 5216, 5224
    wb16 = w.astype(BF16)
    w_main = jnp.concatenate([wb16[:, o_g:o_g + 4096], wb16[:, :3072], wb16[:, o_q:o_q + 1024],
                              wb16[:, o_ckv:o_ckv + 256]], axis=1)
    w_small = jnp.concatenate([wb16[:, o_iq:o_iq + 512], _pad_cols(wb16[:, o_zg:o_zg + 160], 256),
                               wb16[:, o_zw:o_zw + 128], _pad_cols(wb16[:, o_ik:o_ik + 72], 128)], axis=1)
    tm = min(1024, T)
    pm = _proj(x2, norm_mix_g[0], w_main, tm=tm, tn=768, name="proj_main")
    pp = _proj(x2, norm_mix_g[0], w_small, tm=tm, tn=512, name="proj_small")

    head_id = jnp.arange(RW_WIDTH) // 64
    bd = (head_id[:, None] == head_id[None, :]).astype(BF16)
    mixes = [sm[:1024], sm[1024:2048], sm[2048:3072], sm[o_zw:o_zw + 128],
             jnp.pad(sm[o_zg:o_zg + 160], (0, 96))]
    wd = jnp.pad(w_decay_up[0], ((0, 64), (0, 0)))
    wi = jnp.pad(w_icl_up[0], ((64, 0), (0, 0)))
    wg = jnp.pad(w_gate_up[0], ((0, 96), (0, 0)))
    r, k4, v, kkn, b, lw, g, bonus = _rw_prep(pm, pp, mixes, w0[0], wd, a0[0], wi, wg, k_k[0], k_a[0],
                                              r_k[0].reshape(-1), bd, tm=min(256, T))
    nch = T // RW_CHUNK
    p_mat, q_mat, rp, y0 = _rw_chunk(r, k4, v, kkn, b, lw, G=min(8, nch))
    y = _rw_scan(p_mat, q_mat, rp, y0)

    wuk_t = jnp.transpose(w_uk[0], (0, 2, 1))
    wuk2 = jnp.stack([_block_diag2(wuk_t[2 * p], wuk_t[2 * p + 1]) for p in range(SA_HEADS // 2)]).astype(BF16)
    wuv2 = jnp.stack([_block_diag2(w_uv[0][2 * p], w_uv[0][2 * p + 1]) for p in range(SA_HEADS // 2)]).astype(BF16)
    c, ql = _dsa_prep(pm, kv_norm_g[0], wuk2, tm=min(512, T))
    o_dsa = _dsa(pp, *_dsa_operands(pp, c), ql, _rel_bias_tiles(rel_bias), wuv2, topk=topk)

    h1 = _merge(y, bonus, g, o_dsa, pm, x2, ln_x_w[0], ln_x_b[0], bd,
                w_branch_rwkv[0].astype(BF16), w_branch_dsa[0].astype(BF16), w_out[0].astype(BF16),
                tm=min(256, T))

    qp = _proj(h1, norm_ffn_g[0], w_peer_query[0].astype(BF16), tm=tm, tn=512, name="peer_query", lane_groups=True)
    keys = peer_sub_keys[0].reshape(2 * PEER_HEADS, PEER_KEYS, LANES)
    et, tau = _peer_select(qp, keys, tm=256)
    u = peer_u[0].astype(BF16)
    vt = jnp.transpose(peer_v[0].astype(BF16))
    out = _peer(h1, norm_ffn_g[0], norm_final_g, u, vt, et, tau, tm=min(512, T), e_tile=512)
    return out[None]
```

```python
import functools
import math

import numpy as np
import jax
import jax.numpy as jnp
from jax import lax
from jax.experimental import pallas as pl
from jax.experimental.pallas import tpu as pltpu

F32, BF16, I32 = jnp.float32, jnp.bfloat16, jnp.int32

D_MODEL = 2048
EPS = 1e-6
RW_WIDTH = 1024
RW_GN_EPS = 64e-5
RW_CHUNK = 128
RW_LOCKSTEP = 8
SA_HEADS = 16
SA_SCALE = 64 ** -0.5
KV_RANK = 256
IDX_HEADS = 8
IDX_SCALE = (8 * 64) ** -0.5
TOPK_MAX = 256
REL_BUCKETS = 32
REL_MAX_DIST = 1024
PEER_HEADS = 8
PEER_KEYS = 128
PEER_TOPK = 16
LANES = 128
SUBLANES = 8
NEG_BIG = -1e30
INT_MIN = -2 ** 31

PM_GATE_A, PM_GATE_B = 0, 1
PM_R, PM_K, PM_V, PM_Q = 4, 5, 6, 7
PM_CKV = 32
PP_ZG = 2
PP_ZWA = 6
PP_IK0, PP_IW0 = 896, 960

Q_BLK = 128
K_TILE = 256
N_BIAS_TILES = 7
R_TILES = 4
LOG2E = math.log2(math.e)


def _cp(sem, vmem_mb=48):
    return pltpu.CompilerParams(dimension_semantics=sem, vmem_limit_bytes=vmem_mb * 1024 * 1024)


def _bf(x):
    return x.astype(BF16)


def _split2(x):
    hi = x.astype(BF16)
    lo = (x - hi.astype(F32)).astype(BF16)
    return hi, lo


def _split2_glue(x):
    hi32 = lax.bitcast_convert_type(lax.bitcast_convert_type(x, jnp.uint32) & jnp.uint32(0xFFFF0000), F32)
    return hi32.astype(BF16), (x - hi32).astype(BF16)


def _mm(a, b):
    return jnp.dot(a, b, preferred_element_type=F32)


def _mm_nt(a, b):
    return lax.dot_general(a, b, (((1,), (1,)), ((), ())), preferred_element_type=F32)


def _mm_tn(a, b):
    return lax.dot_general(a, b, (((0,), (0,)), ((), ())), preferred_element_type=F32)


def _d3(a_s, b_s, mm=_mm):
    return mm(a_s[0], b_s[0]) + (mm(a_s[0], b_s[1]) + mm(a_s[1], b_s[0]))


def _dot3(a, b, mm=_mm):
    return _d3(_split2(a), _split2(b), mm)


def _proj_kernel(x_ref, g_ref, w_ref, o_ref, xn_ref, *, lane_groups):
    @pl.when(pl.program_id(1) == 0)
    def _():
        x = x_ref[...]
        xn_ref[...] = (x * lax.rsqrt(jnp.mean(x * x, axis=-1, keepdims=True) + EPS) * g_ref[...]).astype(BF16)

    res = _mm(xn_ref[...], w_ref[...])
    if lane_groups:
        for k in range(res.shape[1] // LANES):
            o_ref[k] = res[:, k * LANES:(k + 1) * LANES]
    else:
        o_ref[...] = res.astype(o_ref.dtype)


def _proj(x, g, w, *, tm, tn, name, lane_groups=False, out_dtype=F32):
    T, D = x.shape
    N = w.shape[1]
    if lane_groups:
        out_spec = pl.BlockSpec((tn // LANES, tm, LANES), lambda i, j: (j, i, 0))
        out_shape = jax.ShapeDtypeStruct((N // LANES, T, LANES), F32)
    else:
        out_spec = pl.BlockSpec((tm, tn), lambda i, j: (i, j))
        out_shape = jax.ShapeDtypeStruct((T, N), out_dtype)
    return pl.pallas_call(
        functools.partial(_proj_kernel, lane_groups=lane_groups),
        grid=(T // tm, N // tn),
        in_specs=[pl.BlockSpec((tm, D), lambda i, j: (i, 0)),
                  pl.BlockSpec((1, D), lambda i, j: (0, 0)),
                  pl.BlockSpec((D, tn), lambda i, j: (0, j))],
        out_specs=out_spec,
        out_shape=out_shape,
        scratch_shapes=[pltpu.VMEM((tm, D), BF16)],
        compiler_params=_cp(("parallel", "arbitrary")),
        name=name,
    )(x, g.reshape(1, D), w)


def _rw_prep_kernel(r_ref, k_ref, v_ref, wa_ref, zg_ref, rp_ref, kp_ref, vp_ref, wap_ref, zgp_ref,
                    mr_ref, mk_ref, mv_ref, mwa_ref, mg_ref, w0_ref, wd_ref, a0_ref, wi_ref, wg_ref,
                    kk_ref, ka_ref, rk_ref, bd_ref,
                    r_out, k_out, v_out, kkn_out, b_out, lw_out, g_out, bonus_out):
    first = pl.program_id(0) == 0

    def shift(p_ref, prev_ref, mix_ref):
        p = p_ref[...].astype(F32)
        rows = prev_ref.shape[0]
        last = jnp.where(first, 0.0, prev_ref[rows - 1:rows, :].astype(F32))
        row = lax.broadcasted_iota(I32, p.shape, 0)
        prev = jnp.where(row == 0, last, pltpu.roll(p, 1, 0))
        return p + (prev - p) * mix_ref[...]

    r = shift(r_ref, rp_ref, mr_ref)
    k = shift(k_ref, kp_ref, mk_ref)
    v = shift(v_ref, vp_ref, mv_ref)
    zwa = shift(wa_ref, wap_ref, mwa_ref)
    zg = shift(zg_ref, zgp_ref, mg_ref)

    nd = -(w0_ref[...] + _dot3(jnp.tanh(zwa), wd_ref[...]))
    softplus = jnp.maximum(nd, 0.0) + jnp.log1p(jnp.exp(-jnp.abs(nd)))
    lw = -jnp.exp(-softplus - 0.5)
    a = jax.nn.sigmoid(a0_ref[...] + _dot3(zwa, wi_ref[...]))
    g = _dot3(jax.nn.sigmoid(zg), wg_ref[...])

    bd = bd_ref[...]
    wide = bd.shape[0]

    def head_sum(x):
        hi, lo = _split2(x)
        parts = [_mm(hi[:, p * wide:(p + 1) * wide], bd) + _mm(lo[:, p * wide:(p + 1) * wide], bd)
                 for p in range(RW_WIDTH // wide)]
        return jnp.concatenate(parts, axis=1)

    kk = k * kk_ref[...]
    kkn = kk / jnp.maximum(jnp.sqrt(head_sum(kk * kk)), 1e-12)
    k4 = k * (1.0 + (a - 1.0) * ka_ref[...])
    r_out[...] = r.astype(BF16)
    k_out[...] = k4.astype(BF16)
    v_out[...] = v.astype(BF16)
    kkn_out[...] = kkn.astype(BF16)
    b_out[...] = (kkn * a).astype(BF16)
    lw_out[...] = lw
    g_out[...] = g
    bonus_out[...] = head_sum(r * k4 * rk_ref[...]) * v


def _rw_prep(pm, pp, mixes, w0, wd, a0, wi, wg, k_k, k_a, r_k, bd, *, tm):
    T = pm.shape[0]
    W = RW_WIDTH

    def cur(width, col):
        return pl.BlockSpec((tm, width), lambda i: (i, col))

    def prev(width, col, rows=SUBLANES):
        return pl.BlockSpec((rows, width), lambda i: (jnp.maximum(i * (tm // rows) - 1, 0), col))

    def full(shape):
        return pl.BlockSpec(shape, lambda i: (0,) * len(shape))

    row = lambda a: a.reshape(1, -1)
    out_sd = [jax.ShapeDtypeStruct((T, W), BF16)] * 5 + [jax.ShapeDtypeStruct((T, W), F32)] * 3
    pr = 2 * SUBLANES
    return pl.pallas_call(
        _rw_prep_kernel,
        grid=(T // tm,),
        in_specs=[cur(W, PM_R), cur(W, PM_K), cur(W, PM_V), cur(128, PP_ZWA), cur(256, PP_ZG),
                  prev(W, PM_R, pr), prev(W, PM_K, pr), prev(W, PM_V, pr), prev(128, PP_ZWA), prev(256, PP_ZG),
                  full((1, W)), full((1, W)), full((1, W)), full((1, 128)), full((1, 256)),
                  full((1, W)), full((128, W)), full((1, W)), full((128, W)), full((256, W)),
                  full((1, W)), full((1, W)), full((1, W)), full((2 * LANES, 2 * LANES))],
        out_specs=[pl.BlockSpec((tm, W), lambda i: (i, 0))] * 8,
        out_shape=out_sd,
        compiler_params=_cp(("parallel",)),
        name="rw_prep",
    )(pm, pm, pm, pp, pp, pm, pm, pm, pp, pp,
      *[row(m) for m in mixes], row(w0), wd, row(a0), wi, wg, row(k_k), row(k_a), row(r_k), bd[:2 * LANES, :2 * LANES])


def _rw_chunk_kernel(r_ref, k_ref, v_ref, kkn_ref, b_ref, lw_ref, p_out, q_out, rp_out, y0_out, *, G):
    C = RW_CHUNK
    row = lax.broadcasted_iota(I32, (C, C), 0)
    col = lax.broadcasted_iota(I32, (C, C), 1)
    incl = row >= col
    strict = row > col
    eye = row == col
    same_head = (row // 64) == (col // 64)
    tri = jnp.where(incl, 1.0, 0.0).astype(BF16)
    lane = lax.broadcasted_iota(I32, (C, LANES), 1)
    eye_f = jnp.where(eye, 1.0, 0.0)

    head_lanes = (lane < 64, lane >= 64)
    U = RW_LOCKSTEP

    def body(cu, carry):
        ch = [cu * U + u for u in range(U)]
        sls = [pl.ds(pl.multiple_of(c * C, C), C) for c in ch]
        pre = []
        for sl in sls:
            r, k4, v, kkn, b, lw = (ref[sl, :].astype(F32) for ref in (r_ref, k_ref, v_ref, kkn_ref, b_ref, lw_ref))
            h1 = lw.astype(BF16)
            r1 = lw - h1.astype(F32)
            h2 = r1.astype(BF16)
            h3 = (r1 - h2.astype(F32)).astype(BF16)
            L = _mm(tri, h1) + (_mm(tri, h2) + _mm(tri, h3))
            Ltot = L[C - 1:C, :]
            Lr = L - L[C // 2 - 1:C // 2, :]
            eneg = jnp.exp(-Lr)
            tail = jnp.exp(Ltot - L)
            pre.append(dict(
                rhat=r * jnp.exp(Lr), ahat=-kkn * jnp.exp(Lr - lw), kt=_bf(k4 * eneg), bt=_bf(b * eneg),
                rabs=r * jnp.exp(L), aabs=-kkn * jnp.exp(L - lw), kbar=_bf(k4 * tail), bbar=_bf(b * tail),
                gam=jnp.exp(Ltot), vb=_bf(v)))
        inst = [(u, hh) for u in range(U) for hh in range(2)]
        am = [_bf(jnp.where(head_lanes[hh], pre[u]["ahat"], 0.0)) for u, hh in inst]
        rm = [_bf(jnp.where(head_lanes[hh], pre[u]["rhat"], 0.0)) for u, hh in inst]
        a_ab = [jnp.where(strict, _mm_nt(am[k], pre[u]["bt"]), 0.0) for k, (u, hh) in enumerate(inst)]
        a_ak = [_bf(jnp.where(strict, _mm_nt(am[k], pre[u]["kt"]), 0.0)) for k, (u, hh) in enumerate(inst)]
        m_rb = [_bf(jnp.where(incl, _mm_nt(rm[k], pre[u]["bt"]), 0.0)) for k, (u, hh) in enumerate(inst)]
        m_rk = [_bf(jnp.where(incl, _mm_nt(rm[k], pre[u]["kt"]), 0.0)) for k, (u, hh) in enumerate(inst)]
        n = a_ab
        x = [eye_f + t for t in n]
        for _ in range(6):
            nb = [_bf(t) for t in n]
            n = [_mm(t, t) for t in nb]
            x = [xi + _mm(_bf(xi), _bf(ni)) for xi, ni in zip(x, n)]
        xb = [_bf(t) for t in x]
        akv = [_bf(_mm(a_ak[k], pre[u]["vb"])) for k, (u, hh) in enumerate(inst)]
        ah_h = [_mm(xb[k], _bf(jnp.where(head_lanes[hh], pre[u]["aabs"], 0.0))) for k, (u, hh) in enumerate(inst)]
        u0_h = [_mm(xb[k], akv[k]) for k in range(len(inst))]
        y0_h = [_mm(m_rk[k], pre[u]["vb"]) + _mm(m_rb[k], _bf(u0_h[k])) for k, (u, hh) in enumerate(inst)]
        rp_h = [jnp.where(head_lanes[hh], pre[u]["rabs"], 0.0) + _mm(m_rb[k], _bf(ah_h[k]))
                for k, (u, hh) in enumerate(inst)]
        for u in range(U):
            lo, hi = 2 * u, 2 * u + 1
            ah = _bf(ah_h[lo] + ah_h[hi])
            u0 = _bf(jnp.where(head_lanes[0], u0_h[lo], u0_h[hi]))
            pu = pre[u]
            p_out[0, ch[u]] = jnp.where(eye, pu["gam"], 0.0) + jnp.where(same_head, _mm_tn(pu["bbar"], ah), 0.0)
            q_out[0, ch[u]] = jnp.where(same_head, _mm_tn(pu["kbar"], pu["vb"]) + _mm_tn(pu["bbar"], u0), 0.0)
            rp_out[sls[u], :] = rp_h[lo] + rp_h[hi]
            y0_out[sls[u], :] = jnp.where(head_lanes[0], y0_h[lo], y0_h[hi])
        return carry

    lax.fori_loop(0, G // U, body, 0)


def _rw_chunk(r, k4, v, kkn, b, lw, *, G):
    T = r.shape[0]
    C = RW_CHUNK
    nch = T // C
    rows = G * C
    cur = pl.BlockSpec((rows, LANES), lambda p, g: (g, p))
    mat = pl.BlockSpec((1, G, C, C), lambda p, g: (p, g, 0, 0))
    return pl.pallas_call(
        functools.partial(_rw_chunk_kernel, G=G),
        grid=(RW_WIDTH // LANES, nch // G),
        in_specs=[cur] * 6,
        out_specs=[mat, mat, cur, cur],
        out_shape=[jax.ShapeDtypeStruct((RW_WIDTH // LANES, nch, C, C), F32)] * 2
        + [jax.ShapeDtypeStruct((T, RW_WIDTH), F32)] * 2,
        compiler_params=_cp(("parallel", "parallel")),
        name="rw_chunk",
    )(r, k4, v, kkn, b, lw)


def _rw_scan_kernel(p_ref, q_ref, rp_ref, y0_ref, y_out, s_scr):
    @pl.when(pl.program_id(0) == 0)
    def _():
        s_scr[...] = jnp.zeros_like(s_scr)

    for p in range(RW_WIDTH // LANES):
        ls = slice(p * LANES, (p + 1) * LANES)
        s0 = _bf(s_scr[p])
        y_out[:, ls] = _mm(_bf(rp_ref[:, ls]), s0) + y0_ref[:, ls]
        s_scr[p] = _mm(_bf(p_ref[p, 0]), s0) + q_ref[p, 0]


def _rw_scan(p_mat, q_mat, rp, y0):
    T = rp.shape[0]
    C = RW_CHUNK
    npair = RW_WIDTH // LANES
    mat = pl.BlockSpec((npair, 1, C, C), lambda c: (0, c, 0, 0))
    cur = pl.BlockSpec((C, RW_WIDTH), lambda c: (c, 0))
    return pl.pallas_call(
        _rw_scan_kernel,
        grid=(T // C,),
        in_specs=[mat, mat, cur, cur],
        out_specs=cur,
        out_shape=jax.ShapeDtypeStruct((T, RW_WIDTH), F32),
        scratch_shapes=[pltpu.VMEM((npair, C, C), F32)],
        compiler_params=_cp(("arbitrary",)),
        name="rw_scan",
    )(p_mat, q_mat, rp, y0)


def _dsa_prep_kernel(q_ref, ckv_ref, g_ref, wuk_ref, c_out, ql_out):
    ckv = ckv_ref[...].astype(F32)
    c = ckv * lax.rsqrt(jnp.mean(ckv * ckv, axis=-1, keepdims=True) + EPS) * g_ref[...]
    c_out[...] = c.astype(BF16)
    for p in range(SA_HEADS // 2):
        qp = q_ref[:, p * LANES:(p + 1) * LANES].astype(BF16)
        ql = _mm(qp, wuk_ref[p]) * (SA_SCALE * LOG2E)
        ql_out[2 * p] = ql[:, :KV_RANK].astype(BF16)
        ql_out[2 * p + 1] = ql[:, KV_RANK:].astype(BF16)


def _dsa_prep(pm, kv_norm_g, wuk2, *, tm):
    T = pm.shape[0]
    return pl.pallas_call(
        _dsa_prep_kernel,
        grid=(T // tm,),
        in_specs=[pl.BlockSpec((tm, 1024), lambda i: (i, PM_Q)),
                  pl.BlockSpec((tm, KV_RANK), lambda i: (i, PM_CKV)),
                  pl.BlockSpec((1, KV_RANK), lambda i: (0, 0)),
                  pl.BlockSpec((SA_HEADS // 2, LANES, 2 * KV_RANK), lambda i: (0, 0, 0))],
        out_specs=[pl.BlockSpec((tm, KV_RANK), lambda i: (i, 0)),
                   pl.BlockSpec((SA_HEADS, tm, KV_RANK), lambda i: (0, i, 0))],
        out_shape=[jax.ShapeDtypeStruct((T, KV_RANK), BF16),
                   jax.ShapeDtypeStruct((SA_HEADS, T, KV_RANK), BF16)],
        compiler_params=_cp(("parallel",)),
        name="dsa_prep",
    )(pm, pm, kv_norm_g.reshape(1, KV_RANK), wuk2)


def _bucket_bounds():
    nb = REL_BUCKETS // 2
    max_exact = nb // 2
    n = np.arange(1, 2 * REL_MAX_DIST, dtype=np.int64)
    large = max_exact + (np.log(n.astype(np.float32) / max_exact)
                         / math.log(REL_MAX_DIST / max_exact) * (nb - max_exact)).astype(np.int32)
    large = np.minimum(large, nb - 1)
    bounds = [int(n[np.argmax(large >= b)]) for b in range(max_exact + 1, nb)]
    assert all(b2 > b1 for b1, b2 in zip(bounds, bounds[1:]))
    return bounds


_BUCKET_BOUNDS = _bucket_bounds()


def _rel_bias_kernel(rb_ref, bt_out):
    o = pl.program_id(0)
    sl = lax.broadcasted_iota(I32, (K_TILE, Q_BLK), 0)
    tl = lax.broadcasted_iota(I32, (K_TILE, Q_BLK), 1)
    rel = sl - tl - o * Q_BLK
    nb = REL_BUCKETS // 2
    max_exact = nb // 2
    n = jnp.abs(rel)
    large = jnp.full(rel.shape, max_exact, I32)
    for bound in _BUCKET_BOUNDS:
        large = large + jnp.where(n >= bound, 1, 0)
    bucket = jnp.where(rel > 0, nb, 0) + jnp.where(n < max_exact, n, large)
    for h in range(SA_HEADS):
        acc = jnp.zeros(rel.shape, F32)
        for bkt in range(REL_BUCKETS):
            acc = jnp.where(bucket == bkt, rb_ref[bkt, h], acc)
        bt_out[0, h] = (acc - rb_ref[nb - 1, h]) * LOG2E


def _rel_bias_tiles(rel_bias):
    return pl.pallas_call(
        _rel_bias_kernel,
        grid=(N_BIAS_TILES,),
        in_specs=[pl.BlockSpec(memory_space=pltpu.SMEM)],
        out_specs=pl.BlockSpec((1, SA_HEADS, K_TILE, Q_BLK), lambda o: (o, 0, 0, 0)),
        out_shape=jax.ShapeDtypeStruct((N_BIAS_TILES, SA_HEADS, K_TILE, Q_BLK), F32),
        compiler_params=_cp(("parallel",)),
        name="rel_bias",
    )(rel_bias)


def _dsa_kernel(iq_ref, iwt_ref, ql_ref, ik_ref, c_ref, ct_ref, bt_ref, wuv_ref, o_out,
                key_scr, iq_scr, m_scr, alpha_scr, p_scr, acc_scr, *, topk):
    i = pl.program_id(0)
    t0 = i * Q_BLK
    n_tiles = i // 2 + 1
    H = SA_HEADS
    lane_q = lax.broadcasted_iota(I32, (Q_BLK, LANES), 1)
    first = lane_q < 64

    for p in range(IDX_HEADS // 2):
        pair = iq_ref[:, p * LANES:(p + 1) * LANES]
        swap = pltpu.roll(pair, 64, 1)
        rows = []
        for own, other in ((pair, swap), (swap, pair)):
            o_hi, o_lo = _split2(own)
            t_hi = other.astype(BF16)
            rows.append(jnp.concatenate([jnp.where(first, o_hi, t_hi), jnp.where(first, o_lo, jnp.zeros_like(o_lo))], axis=1))
        iq_scr[p] = jnp.concatenate(rows, axis=0)
    w_rows = [iwt_ref[h:h + 1, :] * IDX_SCALE for h in range(IDX_HEADS)]

    key_pos = lax.broadcasted_iota(I32, (K_TILE, Q_BLK), 0)
    q_lane = lax.broadcasted_iota(I32, (K_TILE, Q_BLK), 1)
    limit = t0 + (q_lane // 64 + 1) * 64

    def score_tiles(jj, carry):
        starts = [pl.multiple_of((2 * jj + u) * K_TILE, K_TILE) for u in range(2)]
        dots = [[_mm_nt(ik_ref[pl.ds(s0, K_TILE), :], iq_scr[p]) for p in range(IDX_HEADS // 2)] for s0 in starts]
        for s0, tile_dots in zip(starts, dots):
            sc = jnp.zeros((K_TILE, Q_BLK), F32)
            for p, d in enumerate(tile_dots):
                sc = (sc + jnp.maximum(d[:, :Q_BLK], 0.0) * w_rows[2 * p]
                      + jnp.maximum(d[:, Q_BLK:], 0.0) * w_rows[2 * p + 1])
            sc = sc + 0.0
            key_scr[pl.ds(s0, K_TILE), :] = jnp.where(s0 + key_pos < limit, sc, -jnp.inf)
        return carry

    lax.fori_loop(0, (n_tiles + 1) // 2, score_tiles, 0)

    n_steps = (n_tiles + R_TILES - 1) // R_TILES

    def pad_tile(j, carry):
        key_scr[pl.ds(pl.multiple_of(j * K_TILE, K_TILE), K_TILE), :] = jnp.full((K_TILE, Q_BLK), -jnp.inf, F32)
        return carry

    lax.fori_loop(n_tiles, n_steps * R_TILES, pad_tile, 0)

    def count(pred):
        def step(js, acc):
            s0 = pl.multiple_of(js * (R_TILES * K_TILE), R_TILES * K_TILE)
            for sub in range(R_TILES * K_TILE // 64):
                hit = jnp.where(pred(key_scr[pl.ds(s0 + sub * 64, 64), :]), 1.0, 0.0)
                acc = acc + jnp.sum(hit.reshape(64 // SUBLANES, SUBLANES, Q_BLK), axis=0)
            return acc
        acc = lax.fori_loop(0, n_steps, step, jnp.zeros((SUBLANES, Q_BLK), F32))
        return jnp.sum(acc, axis=0, keepdims=True)

    def as_float(ordered):
        return pltpu.bitcast(ordered ^ ((ordered >> 31) & 0x7FFFFFFF), F32)

    def bit_step(it, cur):
        bit = lax.shift_left(jnp.int32(1), 31 - it)
        cand = as_float((cur | bit) ^ INT_MIN)
        cnt = count(lambda kt: kt >= cand)
        return jnp.where(cnt >= topk, cur | bit, cur)

    cur = lax.fori_loop(0, 32, bit_step, jnp.zeros((1, Q_BLK), I32))
    take_all = limit[:1, :] <= topk
    theta = jnp.where(take_all, -jnp.inf, as_float(cur ^ INT_MIN))
    need = jnp.where(take_all, 0.0, topk - count(lambda kt: kt > theta))

    lr = lax.broadcasted_iota(I32, (K_TILE, K_TILE), 0)
    lc = lax.broadcasted_iota(I32, (K_TILE, K_TILE), 1)
    lower = jnp.where(lr >= lc, 1.0, 0.0).astype(BF16)

    def mask_tiles(jj, carry):
        starts = [pl.multiple_of((2 * jj + u) * K_TILE, K_TILE) for u in range(2)]
        kts = [key_scr[pl.ds(s0, K_TILE), :] for s0 in starts]
        eqs = [jnp.where(kt == theta, 1.0, 0.0) for kt in kts]
        within = [_mm(lower, e.astype(BF16)) for e in eqs]
        for s0, kt, e, w in zip(starts, kts, eqs, within):
            tie = jnp.where(carry + w <= need, 0.0, NEG_BIG)
            addm = jnp.where(kt > theta, 0.0, jnp.where(kt == theta, tie, NEG_BIG))
            key_scr[pl.ds(s0, K_TILE), :] = addm
            carry = carry + jnp.sum(e, axis=0, keepdims=True)
        return carry

    lax.fori_loop(0, (n_tiles + 1) // 2, mask_tiles, jnp.zeros((1, Q_BLK), F32))

    m_scr[...] = jnp.full(m_scr.shape, NEG_BIG, F32)
    acc_scr[...] = jnp.zeros(acc_scr.shape, F32)
    p_scr[...] = jnp.zeros(p_scr.shape, BF16)
    alpha_scr[...] = jnp.ones(alpha_scr.shape, F32)

    def apply_pending(ct_t, ls):
        acc_scr[:, ls] = alpha_scr[:, ls] * acc_scr[:, ls] + _mm(ct_t, p_scr[:, ls])

    def attn_tile(j, carry, *, near):
        s0 = pl.multiple_of(j * K_TILE, K_TILE)
        c_t = c_ref[pl.ds(s0, K_TILE), :]
        ct_prev = ct_ref[:, pl.ds(pl.multiple_of(jnp.maximum(j - 1, 0) * K_TILE, K_TILE), K_TILE)]
        addm = key_scr[pl.ds(s0, K_TILE), :]
        addm2 = jnp.concatenate([addm, addm], axis=1)

        def pair_logits(p):
            return _mm_nt(c_t, ql_ref[2 * p:2 * p + 2].reshape(2 * Q_BLK, KV_RANK))

        lg2_next = pair_logits(0)
        for p in range(H // 2):
            lg2 = lg2_next
            if p + 1 < H // 2:
                lg2_next = pair_logits(p + 1)
            apply_pending(ct_prev, slice(2 * p * Q_BLK, (2 * p + 2) * Q_BLK))
            for hh in range(2):
                h = 2 * p + hh
                ls = slice(h * Q_BLK, (h + 1) * Q_BLK)
                lg = lg2[:, hh * Q_BLK:(hh + 1) * Q_BLK] + addm
                if near:
                    lg = lg + bt_ref[i - 2 * j, h]
                m_prev = m_scr[:, ls]
                m_new = jnp.maximum(m_prev, jnp.max(lg, axis=0, keepdims=True))
                m_scr[:, ls] = m_new
                alpha_scr[:, ls] = jnp.exp2(m_prev - m_new)
                p_scr[:, ls] = jnp.exp2(lg - m_new).astype(BF16)
        return carry

    n_far = jnp.maximum(i - N_BIAS_TILES + 2, 0) // 2
    lax.fori_loop(0, n_far, functools.partial(attn_tile, near=False), 0)
    lax.fori_loop(n_far, n_tiles, functools.partial(attn_tile, near=True), 0)
    ct_last = ct_ref[:, pl.ds(pl.multiple_of((n_tiles - 1) * K_TILE, K_TILE), K_TILE)]
    for p in range(H // 2):
        apply_pending(ct_last, slice(2 * p * Q_BLK, (2 * p + 2) * Q_BLK))

    o_lat = (acc_scr[:KV_RANK, :] / acc_scr[KV_RANK:KV_RANK + 1, :]).T.astype(BF16)
    for p in range(H // 2):
        pair = jnp.concatenate([o_lat[2 * p * Q_BLK:(2 * p + 1) * Q_BLK], o_lat[(2 * p + 1) * Q_BLK:(2 * p + 2) * Q_BLK]],
                               axis=1)
        o_out[:, p * LANES:(p + 1) * LANES] = _mm(pair, wuv_ref[p]).astype(o_out.dtype)


ONES_ROWS = 16


def _dsa_operands(pp, c):
    ik = pp[:, PP_IK0:PP_IK0 + 64]
    ik_hi, ik_lo = _split2_glue(ik)
    ik_packed = jnp.concatenate([ik_hi, ik_lo, ik_hi, jnp.zeros_like(ik_hi)], axis=1)
    iwt = jnp.transpose(pp[:, PP_IW0:PP_IW0 + IDX_HEADS])
    ct_ext = jnp.concatenate([jnp.transpose(c), jnp.ones((ONES_ROWS, c.shape[0]), c.dtype)], axis=0)
    return iwt, ik_packed, c, ct_ext


def _dsa(pp, iwt, ik_packed, c, ct_ext, ql, bt, wuv2, *, topk):
    T = pp.shape[0]
    H = SA_HEADS
    assert T % (R_TILES * K_TILE) == 0 and N_BIAS_TILES * Q_BLK - (K_TILE - 1) >= _BUCKET_BOUNDS[-1]
    whole = pl.BlockSpec(memory_space=pltpu.VMEM)
    return pl.pallas_call(
        functools.partial(_dsa_kernel, topk=topk),
        grid=(T // Q_BLK,),
        in_specs=[pl.BlockSpec((Q_BLK, 512), lambda i: (i, 0)),
                  pl.BlockSpec((IDX_HEADS, Q_BLK), lambda i: (0, i)),
                  pl.BlockSpec((H, Q_BLK, KV_RANK), lambda i: (0, i, 0)),
                  whole, whole, whole, whole, whole],
        out_specs=pl.BlockSpec((Q_BLK, 1024), lambda i: (i, 0)),
        out_shape=jax.ShapeDtypeStruct((T, 1024), BF16),
        scratch_shapes=[pltpu.VMEM((T, Q_BLK), F32),
                        pltpu.VMEM((IDX_HEADS // 2, 2 * Q_BLK, 2 * LANES), BF16),
                        pltpu.VMEM((1, H * Q_BLK), F32),
                        pltpu.VMEM((1, H * Q_BLK), F32),
                        pltpu.VMEM((K_TILE, H * Q_BLK), BF16),
                        pltpu.VMEM((KV_RANK + ONES_ROWS, H * Q_BLK), F32)],
        compiler_params=_cp(("parallel",), 56),
        name="dsa",
    )(pp, iwt, ql, ik_packed, c, ct_ext, bt, wuv2)


def _merge_kernel(y_ref, bonus_ref, g_ref, o_ref, ga_ref, gb_ref, x_ref, lnw_ref, lnb_ref, bd_ref,
                  wa_ref, wb_ref, wout_ref, h_out):
    bd = bd_ref[...]
    wide = bd.shape[0]

    def head_mean(x):
        hi, lo = _split2(x)
        parts = [_mm(hi[:, p * wide:(p + 1) * wide], bd) + _mm(lo[:, p * wide:(p + 1) * wide], bd)
                 for p in range(RW_WIDTH // wide)]
        return jnp.concatenate(parts, axis=1) * (1.0 / 64)

    y = y_ref[...]
    yc = y - head_mean(y)
    yn = yc * lax.rsqrt(head_mean(yc * yc) + RW_GN_EPS) * lnw_ref[...] + lnb_ref[...]
    ya = (yn + bonus_ref[...]) * g_ref[...]
    pa = _mm(ya.astype(BF16), wa_ref[...])
    pb = _mm(o_ref[...].astype(BF16), wb_ref[...])
    m = jax.nn.sigmoid(ga_ref[...].astype(F32)) * pa + jax.nn.sigmoid(gb_ref[...].astype(F32)) * pb
    h_out[...] = x_ref[...] + _mm(m.astype(BF16), wout_ref[...])


def _merge(y, bonus, g, o, pm, x, ln_w, ln_b, bd, wa, wb, wout, *, tm):
    T, D = x.shape
    W = RW_WIDTH
    row = lambda a: a.reshape(1, -1)
    act = pl.BlockSpec((tm, W), lambda i: (i, 0))
    full = lambda shape: pl.BlockSpec(shape, lambda i: (0,) * len(shape))
    whole = pl.BlockSpec(memory_space=pltpu.VMEM)
    return pl.pallas_call(
        _merge_kernel,
        grid=(T // tm,),
        in_specs=[act, act, act, act,
                  pl.BlockSpec((tm, D), lambda i: (i, PM_GATE_A)),
                  pl.BlockSpec((tm, D), lambda i: (i, PM_GATE_B)),
                  pl.BlockSpec((tm, D), lambda i: (i, 0)),
                  full((1, W)), full((1, W)), whole, whole, whole, whole],
        out_specs=pl.BlockSpec((tm, D), lambda i: (i, 0)),
        out_shape=jax.ShapeDtypeStruct((T, D), F32),
        compiler_params=_cp(("parallel",), 56),
        name="merge",
    )(y, bonus, g, o, pm, pm, x, row(ln_w), row(ln_b), bd[:2 * LANES, :2 * LANES], wa, wb, wout)


def _sort16_pairs():
    n, pairs, p = 16, [], 1
    while p < n:
        k = p
        while k >= 1:
            for j in range(k % p, n - k, 2 * k):
                for i in range(min(k, n - j - k)):
                    if (i + j) // (2 * p) == (i + j + k) // (2 * p):
                        pairs.append((i + j, i + j + k))
            k //= 2
        p *= 2
    return pairs


_SORT16 = _sort16_pairs()
PEER_MARGIN = 1.0 - 2.0 ** -20


def _exchange(a, i, j):
    a[i], a[j] = jnp.maximum(a[i], a[j]), jnp.minimum(a[i], a[j])


def _bitonic_sort16(a):
    for dist in (8, 4, 2, 1):
        for i in range(PEER_TOPK):
            if not i & dist:
                _exchange(a, i, i + dist)
    return a


def _top16_merge(x, y):
    return _bitonic_sort16([jnp.maximum(x[i], y[PEER_TOPK - 1 - i]) for i in range(PEER_TOPK)])


def _peer_select_kernel(qp_ref, keys_ref, et_out, tau_out):
    tm = qp_ref.shape[1]
    sub = lax.broadcasted_iota(I32, (SUBLANES, tm), 0)
    zero = jnp.zeros((SUBLANES, tm), F32)

    def per_head(h, packed):
        packed = list(packed)
        for c in range(2):
            hc = 2 * h + c
            s_t = _dot3(keys_ref[hc], qp_ref[hc], _mm_nt)
            e = jnp.exp(s_t - jnp.max(s_t, axis=0, keepdims=True))
            et_out[hc] = e
            a = [e[g * SUBLANES:(g + 1) * SUBLANES, :] for g in range(PEER_KEYS // SUBLANES)]
            for i, j in _SORT16:
                _exchange(a, i, j)
            for shift in (4, 2, 1):
                a = _top16_merge(a, [pltpu.roll(t, shift, 0) for t in a])
            for i in range(PEER_TOPK):
                packed[c * PEER_TOPK + i] = jnp.where(sub == h, a[i], packed[c * PEER_TOPK + i])
        return tuple(packed)

    packed = lax.fori_loop(0, PEER_HEADS, per_head, (zero,) * (2 * PEER_TOPK))
    top_a, top_b = packed[:PEER_TOPK], packed[PEER_TOPK:]

    def run(i, n):
        return [top_a[i] * top_b[j] for j in range(n)]

    g1 = run(0, 16)
    g2 = _bitonic_sort16(run(1, 8) + [top_a[i] * top_b[0] for i in range(PEER_TOPK - 1, 7, -1)])
    g3 = run(2, 5) + run(3, 4) + run(4, 3) + run(5, 2) + run(6, 2)
    for i, j in _SORT16:
        _exchange(g3, i, j)
    top = _top16_merge(_top16_merge(_top16_merge(g1, g2), g3), run(7, 2) + [zero] * 14)
    z = top[0]
    for t in top[1:]:
        z = z + t
    zinv = 1.0 / z
    tau_out[...] = top[PEER_TOPK - 1] * PEER_MARGIN * zinv
    for h in range(PEER_HEADS):
        et_out[2 * h] = et_out[2 * h] * zinv[h:h + 1, :]


def _peer_select(qp, keys, *, tm):
    T = qp.shape[1]
    return pl.pallas_call(
        _peer_select_kernel,
        grid=(T // tm,),
        in_specs=[pl.BlockSpec((2 * PEER_HEADS, tm, LANES), lambda i: (0, i, 0)),
                  pl.BlockSpec((2 * PEER_HEADS, PEER_KEYS, LANES), lambda i: (0, 0, 0))],
        out_specs=[pl.BlockSpec((2 * PEER_HEADS, PEER_KEYS, tm), lambda i: (0, 0, i)),
                   pl.BlockSpec((SUBLANES, tm), lambda i: (0, i))],
        out_shape=[jax.ShapeDtypeStruct((2 * PEER_HEADS, PEER_KEYS, T), F32),
                   jax.ShapeDtypeStruct((PEER_HEADS, T), F32)],
        compiler_params=_cp(("parallel",)),
        name="peer_select",
    )(qp, keys)


def _peer_kernel(h_ref, gf_ref, gout_ref, u_ref, vt_ref, et_ref, tau_ref, out_ref, hn_scr, acc_scr, ga0_scr, ga1_scr,
                 *, e_tile):
    j = pl.program_id(1)
    n_tiles = pl.num_programs(1) - 1
    tm = h_ref.shape[0]
    per_tile = e_tile // PEER_KEYS

    @pl.when(j == 0)
    def _():
        h = h_ref[...]
        hn = h * lax.rsqrt(jnp.mean(h * h, axis=-1, keepdims=True) + EPS) * gf_ref[...]
        hn_scr[...] = hn.astype(BF16)
        acc_scr[...] = jnp.zeros_like(acc_scr)
        ga1_scr[...] = jnp.zeros_like(ga1_scr)

    d_model = vt_ref.shape[0]
    half = e_tile // 2

    def pre_act(s):
        return _mm_nt(u_ref[s * half:(s + 1) * half, :], hn_scr[...])

    def apply_values(q, ga_read):
        rows = slice(q * (d_model // 4), (q + 1) * (d_model // 4))
        acc_scr[rows, :] += _mm(vt_ref[rows, :], ga_read[...])

    RC = 32

    def gated(s, pre, ga_write):
        for ai in range(per_tile // 2):
            a = jnp.minimum(j, n_tiles - 1) * per_tile + s * (per_tile // 2) + ai
            e1 = [et_ref[2 * h, pl.ds(a, 1), :] for h in range(PEER_HEADS)]
            for rc in range(PEER_KEYS // RC):
                x = pre[ai * PEER_KEYS + rc * RC:ai * PEER_KEYS + (rc + 1) * RC, :]
                act = 0.5 * x * (1.0 + lax.erf(x * (2.0 ** -0.5)))
                gate = jnp.zeros((RC, tm), F32)
                for h in range(PEER_HEADS):
                    w = et_ref[2 * h + 1, rc * RC:(rc + 1) * RC, :] * e1[h]
                    gate = gate + jnp.where(w >= tau_ref[h:h + 1, :], w, 0.0)
                r0 = s * half + ai * PEER_KEYS + rc * RC
                ga_write[r0:r0 + RC, :] = (gate * act).astype(BF16)

    def step(ga_write, ga_read):
        pre0 = pre_act(0)
        apply_values(0, ga_read)
        pre1 = pre_act(1)
        gated(0, pre0, ga_write)
        apply_values(1, ga_read)
        apply_values(2, ga_read)
        apply_values(3, ga_read)
        gated(1, pre1, ga_write)

    @pl.when(j % 2 == 0)
    def _():
        step(ga0_scr, ga1_scr)

    @pl.when(j % 2 == 1)
    def _():
        step(ga1_scr, ga0_scr)

    @pl.when(j == n_tiles)
    def _():
        h3 = h_ref[...] + acc_scr[...].T
        out_ref[...] = h3 * lax.rsqrt(jnp.mean(h3 * h3, axis=-1, keepdims=True) + EPS) * gout_ref[...]


def _peer(h1, g_ffn, g_out, u, vt, et, tau, *, tm, e_tile):
    T, D = h1.shape
    E = u.shape[0]
    n_tiles = E // e_tile
    assert n_tiles % 2 == 0
    return pl.pallas_call(
        functools.partial(_peer_kernel, e_tile=e_tile),
        grid=(T // tm, n_tiles + 1),
        in_specs=[pl.BlockSpec((tm, D), lambda i, j: (i, 0)),
                  pl.BlockSpec((1, D), lambda i, j: (0, 0)),
                  pl.BlockSpec((1, D), lambda i, j: (0, 0)),
                  pl.BlockSpec((e_tile, D), lambda i, j: (jnp.minimum(j, n_tiles - 1), 0)),
                  pl.BlockSpec((D, e_tile), lambda i, j: (0, jnp.maximum(j - 1, 0))),
                  pl.BlockSpec((2 * PEER_HEADS, PEER_KEYS, tm), lambda i, j: (0, 0, i)),
                  pl.BlockSpec((PEER_HEADS, tm), lambda i, j: (0, i))],
        out_specs=pl.BlockSpec((tm, D), lambda i, j: (i, 0)),
        out_shape=jax.ShapeDtypeStruct((T, D), F32),
        scratch_shapes=[pltpu.VMEM((tm, D), BF16), pltpu.VMEM((D, tm), F32),
                        pltpu.VMEM((e_tile, tm), BF16), pltpu.VMEM((e_tile, tm), BF16)],
        compiler_params=_cp(("parallel", "arbitrary"), 56),
        name="peer",
    )(h1, g_ffn.reshape(1, D), g_out.reshape(1, D), u, vt, et, tau)


def _pad_cols(a, n):
    return jnp.pad(a, ((0, 0), (0, n - a.shape[1])))


def _block_diag2(a, b):
    za = jnp.zeros((a.shape[0], b.shape[1]), a.dtype)
    zb = jnp.zeros((b.shape[0], a.shape[1]), a.dtype)
    return jnp.concatenate([jnp.concatenate([a, za], axis=1), jnp.concatenate([zb, b], axis=1)], axis=0)


def kernel(x, norm_mix_g, w_in, shift_mix, w0, w_decay_up, a0, w_icl_up, w_gate_up, k_k, k_a, r_k, ln_x_w, ln_x_b,
           kv_norm_g, w_uk, w_uv, rel_bias, w_branch_rwkv, w_branch_dsa, w_out, norm_ffn_g, w_peer_query,
           peer_sub_keys, peer_u, peer_v, norm_final_g):
    B, T, D = x.shape
    assert B == 1 and D == D_MODEL and T % (2 * K_TILE) == 0 and norm_mix_g.shape[0] == 1
    topk = min(TOPK_MAX, T // 4)
    x2 = x[0]
    w = w_in[0]
    sm = shift_mix[0]
    o_zw, o_za, o_zg, o_q, o_ckv, o_iq, o_ik, o_iw, o_g = 3072, 3136, 3200, 3360, 4384, 4640, 5152, 5216, 5224
    wb16 = w.astype(BF16)
    w_main = jnp.concatenate([wb16[:, o_g:o_g + 4096], wb16[:, :3072], wb16[:, o_q:o_q + 1024],
                              wb16[:, o_ckv:o_ckv + 256]], axis=1)
    w_small = jnp.concatenate([wb16[:, o_iq:o_iq + 512], _pad_cols(wb16[:, o_zg:o_zg + 160], 256),
                               wb16[:, o_zw:o_zw + 128], _pad_cols(wb16[:, o_ik:o_ik + 72], 128)], axis=1)
    tm = min(1024, T)
    pm = _proj(x2, norm_mix_g[0], w_main, tm=tm, tn=768, name="proj_main", out_dtype=BF16)
    pp = _proj(x2, norm_mix_g[0], w_small, tm=tm, tn=512, name="proj_small")

    head_id = jnp.arange(RW_WIDTH) // 64
    bd = (head_id[:, None] == head_id[None, :]).astype(BF16)
    mixes = [sm[:1024], sm[1024:2048], sm[2048:3072], sm[o_zw:o_zw + 128],
             jnp.pad(sm[o_zg:o_zg + 160], (0, 96))]
    wd = jnp.pad(w_decay_up[0], ((0, 64), (0, 0)))
    wi = jnp.pad(w_icl_up[0], ((64, 0), (0, 0)))
    wg = jnp.pad(w_gate_up[0], ((0, 96), (0, 0)))
    r, k4, v, kkn, b, lw, g, bonus = _rw_prep(pm, pp, mixes, w0[0], wd, a0[0], wi, wg, k_k[0], k_a[0],
                                              r_k[0].reshape(-1), bd, tm=min(256, T))
    nch = T // RW_CHUNK
    p_mat, q_mat, rp, y0 = _rw_chunk(r, k4, v, kkn, b, lw, G=min(8, nch))
    y = _rw_scan(p_mat, q_mat, rp, y0)

    wuk_t = jnp.transpose(w_uk[0], (0, 2, 1))
    wuk2 = jnp.stack([_block_diag2(wuk_t[2 * p], wuk_t[2 * p + 1]) for p in range(SA_HEADS // 2)]).astype(BF16)
    wuv2 = jnp.stack([_block_diag2(w_uv[0][2 * p], w_uv[0][2 * p + 1]) for p in range(SA_HEADS // 2)]).astype(BF16)
    c, ql = _dsa_prep(pm, kv_norm_g[0], wuk2, tm=min(512, T))
    o_dsa = _dsa(pp, *_dsa_operands(pp, c), ql, _rel_bias_tiles(rel_bias), wuv2, topk=topk)

    h1 = _merge(y, bonus, g, o_dsa, pm, x2, ln_x_w[0], ln_x_b[0], bd,
                w_branch_rwkv[0].astype(BF16), w_branch_dsa[0].astype(BF16), w_out[0].astype(BF16),
                tm=min(256, T))

    qp = _proj(h1, norm_ffn_g[0], w_peer_query[0].astype(BF16), tm=tm, tn=512, name="peer_query", lane_groups=True)
    keys = peer_sub_keys[0].reshape(2 * PEER_HEADS, PEER_KEYS, LANES)
    et, tau = _peer_select(qp, keys, tm=256)
    u = peer_u[0].astype(BF16)
    vt = jnp.transpose(peer_v[0].astype(BF16))
    out = _peer(h1, norm_ffn_g[0], norm_final_g, u, vt, et, tau, tm=min(512, T), e_tile=512)
    return out[None]
```
